```python
import math
import jax
import jax.numpy as jnp
from jax import lax
import numpy as np

D_MODEL = 1024
BATCH = 8
SEQ = 2048
DEPTH = 4
DEC_BATCH = 32
DEC_SEQ = 4
PAST_LEN = 8192
PAGE_SIZE = 128

HEAD_DIM = 64
H_A = D_MODEL // (2 * HEAD_DIM)
KV_A = 2
IDX_HEADS = 8
IDX_DIM = 64
TOPK_MAX = 256
H_B = D_MODEL // (2 * HEAD_DIM)
H_C = D_MODEL // HEAD_DIM
N_BUCKETS = 32
MAX_DISTANCE = 128
N_MEM = 256
XA_HEADS = 4
XA_HEAD_DIM = 128
N_EXPERTS = 32
TOP_K = 4
D_EXPERT = D_MODEL
SWIGLU_LIMIT = 7.0
SWIGLU_ALPHA = 1.702
FORGET_BIAS = 3.0
Q_BLOCK = 128
MOE_BLOCK = 128
LN_EPS = 1e-5
NEG_INF = -1e30
ATTN_SCALE = HEAD_DIM ** -0.5
DN_ALPHA = (2 * DEPTH) ** 0.25
DN_BETA = (8 * DEPTH) ** -0.25
N_EVEN = (DEPTH + 1) // 2
N_ODD = DEPTH // 2
AB_SIZES = (H_A * HEAD_DIM, KV_A * HEAD_DIM, KV_A * HEAD_DIM, IDX_HEADS * IDX_DIM, IDX_DIM, IDX_HEADS,
            H_B * HEAD_DIM, H_B * HEAD_DIM, H_B * HEAD_DIM, H_B)
D_AB = sum(AB_SIZES)

kernel_name = 'hybrid_dsa_fox_stickbreak_moe_step'


def layer_norm(x, g, b):
    xf = x.astype(jnp.float32)
    mu = xf.mean(-1, keepdims=True)
    var = jnp.square(xf - mu).mean(-1, keepdims=True)
    return ((xf - mu) * lax.rsqrt(var + LN_EPS) * g.astype(jnp.float32) + b.astype(jnp.float32)).astype(x.dtype)


def t5_bucket(rel):
    n = jnp.maximum(rel, 0)
    max_exact = N_BUCKETS // 2
    nf = jnp.maximum(n, 1).astype(jnp.float32)
    large = max_exact + (jnp.log(nf / max_exact) / math.log(MAX_DISTANCE / max_exact)
                         * (N_BUCKETS - max_exact)).astype(jnp.int32)
    large = jnp.minimum(large, N_BUCKETS - 1)
    return jnp.where(n < max_exact, n, large)


def q_sweep(fn, q_args):
    n_q = q_args[0].shape[1]
    qb = Q_BLOCK if n_q % Q_BLOCK == 0 else n_q
    nb = n_q // qb

    def split(a):
        return jnp.moveaxis(a.reshape(a.shape[0], nb, qb, *a.shape[2:]), 1, 0)

    out = lax.map(lambda a: fn(*a), tuple(split(a) for a in q_args))
    out = jnp.moveaxis(out, 0, 1)
    return out.reshape(out.shape[0], n_q, *out.shape[3:])


def gather_pages(cache_layer, page_table):
    g = cache_layer[page_table]
    return g.reshape(g.shape[0], g.shape[1] * g.shape[2], *g.shape[3:])


def dsa_attend(q, iq, iw, qpos, k, v, ik, rel_bias):
    bsz = q.shape[0]
    n_keys = k.shape[1]
    n_sel = min(TOPK_MAX, n_keys // 4)
    kpos = jnp.arange(n_keys, dtype=jnp.int32)
    ikf = ik.astype(jnp.float32)
    tab = rel_bias.astype(jnp.float32)
    rep = H_A // KV_A

    def block(qb, iqb, iwb, pb):
        nq = qb.shape[1]
        dots = jnp.einsum('btnd,bsd->btns', iqb.astype(jnp.float32), ikf)
        score = jnp.einsum('btns,btn->bts', jax.nn.relu(dots), iwb.astype(jnp.float32))
        score = jnp.where(kpos[None, None, :] <= pb[:, :, None], score, -jnp.inf)
        _, sel = lax.top_k(score, n_sel)
        ks = jax.vmap(lambda a, i: a[i])(k, sel).astype(jnp.float32)
        vs = jax.vmap(lambda a, i: a[i])(v, sel).astype(jnp.float32)
        qg = qb.astype(jnp.float32).reshape(bsz, nq, KV_A, rep, HEAD_DIM)
        logits = jnp.einsum('btgrd,btkgd->btgrk', qg, ks) * ATTN_SCALE
        rel = pb[:, :, None] - sel
        bias = tab[t5_bucket(rel)].reshape(bsz, nq, n_sel, KV_A, rep).transpose(0, 1, 3, 4, 2)
        valid = (rel >= 0)[:, :, None, None, :]
        p = jax.nn.softmax(jnp.where(valid, logits + bias, NEG_INF), axis=-1)
        o = jnp.einsum('btgrk,btkgd->btgrd', p, vs)
        return o.reshape(bsz, nq, H_A * HEAD_DIM)

    return q_sweep(block, (q, iq, iw, qpos))


def fox_attend(q, cum_q, qpos, k, v, cum_k):
    n_keys = k.shape[1]
    kpos = jnp.arange(n_keys, dtype=jnp.int32)
    kf = k.astype(jnp.float32)
    vf = v.astype(jnp.float32)
    ck = jnp.swapaxes(cum_k, 1, 2)

    def block(qb, cqb, pb):
        logits = jnp.einsum('bthd,bshd->bhts', qb.astype(jnp.float32), kf) * ATTN_SCALE
        logits = logits + jnp.swapaxes(cqb, 1, 2)[..., None] - ck[:, :, None, :]
        mask = kpos[None, None, None, :] <= pb[:, None, :, None]
        p = jax.nn.softmax(jnp.where(mask, logits, NEG_INF), axis=-1)
        o = jnp.einsum('bhts,bshd->bthd', p, vf)
        return o.reshape(o.shape[0], o.shape[1], H_B * HEAD_DIM)

    return q_sweep(block, (q, cum_q, qpos))


def stick_breaking_attend(q, qpos, k, v):
    n_keys = k.shape[1]
    kpos = jnp.arange(n_keys, dtype=jnp.int32)
    kf = k.astype(jnp.float32)
    vf = v.astype(jnp.float32)

    def block(qb, pb):
        z = jnp.einsum('bthd,bshd->bhts', qb.astype(jnp.float32), kf) * ATTN_SCALE
        mask = kpos[None, None, None, :] < pb[:, None, :, None]
        log_keep = jnp.where(mask, jax.nn.log_sigmoid(-z), 0.0)
        after = lax.cumsum(log_keep, axis=3, reverse=True) - log_keep
        w = jnp.where(mask, jnp.exp(jax.nn.log_sigmoid(z) + after), 0.0)
        o = jnp.einsum('bhts,bshd->bthd', w, vf)
        return o.reshape(o.shape[0], o.shape[1], H_C * HEAD_DIM)

    return q_sweep(block, (q, qpos))


def ab_mixer(x, past, w_in, b_f, w_out, rel_bias):
    bsz, t, _ = x.shape
    pts = [int(p) for p in np.cumsum(AB_SIZES)[:-1]]
    qa, ka, va, iq, ik, iw, qb, kb, vb, fl = jnp.split(x @ w_in, pts, axis=-1)
    qa = qa.reshape(bsz, t, H_A, HEAD_DIM)
    ka = ka.reshape(bsz, t, KV_A, HEAD_DIM)
    va = va.reshape(bsz, t, KV_A, HEAD_DIM)
    iq = iq.reshape(bsz, t, IDX_HEADS, IDX_DIM)
    qb = qb.reshape(bsz, t, H_B, HEAD_DIM)
    kb = kb.reshape(bsz, t, H_B, HEAD_DIM)
    vb = vb.reshape(bsz, t, H_B, HEAD_DIM)
    logf = jax.nn.log_sigmoid(fl.astype(jnp.float32) + b_f.astype(jnp.float32)).astype(x.dtype)
    new_rows = (ka, va, ik, kb, vb, logf)
    if past is None:
        full = new_rows
    else:
        full = tuple(jnp.concatenate([p, n], axis=1) for p, n in zip(past, new_rows))
    ka_f, va_f, ik_f, kb_f, vb_f, lf_f = full
    n_keys = ka_f.shape[1]
    qpos = jnp.broadcast_to(n_keys - t + jnp.arange(t, dtype=jnp.int32), (bsz, t))
    o_a = dsa_attend(qa, iq, iw, qpos, ka_f, va_f, ik_f, rel_bias)
    cum = jnp.cumsum(lf_f.astype(jnp.float32), axis=1)
    o_b = fox_attend(qb, cum[:, n_keys - t:], qpos, kb_f, vb_f, cum)
    o = jnp.concatenate([o_a, o_b], axis=-1).astype(x.dtype)
    return o @ w_out, new_rows


def sb_mixer(x, past, w_in, w_out):
    bsz, t, _ = x.shape
    q, k, v = jnp.split(x @ w_in, 3, axis=-1)
    q = q.reshape(bsz, t, H_C, HEAD_DIM)
    k = k.reshape(bsz, t, H_C, HEAD_DIM)
    v = v.reshape(bsz, t, H_C, HEAD_DIM)
    if past is None:
        k_f, v_f = k, v
    else:
        k_f = jnp.concatenate([past[0], k], axis=1)
        v_f = jnp.concatenate([past[1], v], axis=1)
    n_keys = k_f.shape[1]
    qpos = jnp.broadcast_to(n_keys - t + jnp.arange(t, dtype=jnp.int32), (bsz, t))
    o = stick_breaking_attend(q, qpos, k_f, v_f).astype(x.dtype)
    return o @ w_out, (k, v)


def mem_project(mem, w_kv):
    bsz, m, _ = mem.shape
    k, v = jnp.split(mem @ w_kv, 2, axis=-1)
    return (k.reshape(bsz, m, XA_HEADS, XA_HEAD_DIM), v.reshape(bsz, m, XA_HEADS, XA_HEAD_DIM))


def mem_cross_attend(x, w_q, mk, mv, w_o):
    bsz, t, _ = x.shape
    q = (x @ w_q).reshape(bsz, t, XA_HEADS, XA_HEAD_DIM).astype(jnp.float32)
    logits = jnp.einsum('bthd,bmhd->bhtm', q, mk.astype(jnp.float32)) * XA_HEAD_DIM ** -0.5
    p = jax.nn.softmax(logits, axis=-1)
    o = jnp.einsum('bhtm,bmhd->bthd', p, mv.astype(jnp.float32)).reshape(bsz, t, XA_HEADS * XA_HEAD_DIM)
    return o.astype(x.dtype) @ w_o


def moe(x, w_router, b_router, w_gu, b_gu, w_down, b_down):
    bsz, t, d = x.shape
    xt = x.reshape(-1, d)
    n_tok = xt.shape[0]
    logits = (xt @ w_router).astype(jnp.float32) + b_router.astype(jnp.float32)
    top_val, top_idx = lax.top_k(logits, TOP_K)
    gate = jax.nn.softmax(top_val, axis=-1)
    flat_e = top_idx.reshape(-1).astype(jnp.int32)
    n_assign = n_tok * TOP_K
    order = jnp.argsort(flat_e).astype(jnp.int32)
    e_sorted = flat_e[order]
    counts = jnp.bincount(flat_e, length=N_EXPERTS).astype(jnp.int32)
    padded = (counts + MOE_BLOCK - 1) // MOE_BLOCK * MOE_BLOCK
    pad_end = jnp.cumsum(padded)
    pad_start = pad_end - padded
    start = jnp.cumsum(counts) - counts
    dest = pad_start[e_sorted] + jnp.arange(n_assign, dtype=jnp.int32) - start[e_sorted]
    n_blocks = -(-n_assign // MOE_BLOCK) + N_EXPERTS
    n_rows = n_blocks * MOE_BLOCK
    row_tok = jnp.full((n_rows,), n_tok, jnp.int32).at[dest].set(order // TOP_K)
    row_gate = jnp.zeros((n_rows,), jnp.float32).at[dest].set(gate.reshape(-1)[order])
    block_start = jnp.arange(n_blocks, dtype=jnp.int32) * MOE_BLOCK
    block_exp = jnp.minimum(jnp.searchsorted(pad_end, block_start, side='right'), N_EXPERTS - 1)
    x_pad = jnp.concatenate([xt, jnp.zeros((1, d), xt.dtype)], axis=0)
    x_rows = x_pad[row_tok].reshape(n_blocks, MOE_BLOCK, d)

    def expert_block(args):
        xb, e = args
        h = (xb @ w_gu[e] + b_gu[e]).astype(jnp.float32)
        g, u = jnp.split(h, 2, axis=-1)
        g = jnp.minimum(g, SWIGLU_LIMIT)
        u = jnp.clip(u, -SWIGLU_LIMIT, SWIGLU_LIMIT)
        act = (g * jax.nn.sigmoid(SWIGLU_ALPHA * g) * (u + 1.0)).astype(xb.dtype)
        return act @ w_down[e] + b_down[e]

    y_rows = lax.map(expert_block, (x_rows, block_exp)).reshape(n_rows, d)
    y = jnp.zeros((n_tok + 1, d), jnp.float32).at[row_tok].add(y_rows.astype(jnp.float32) * row_gate[:, None])
    return y[:n_tok].reshape(bsz, t, d).astype(x.dtype)


def run_trunk(x, mem_k, mem_v, paged, page_table, w_in_ab, b_forget, w_out_ab, w_in_c, w_out_c, rel_bias,
              w_xq, w_xo, ln_g, ln_b, w_router, b_router, w_gate_up, b_gate_up, w_down, b_down):
    rows_ab, rows_c = [], []
    for li in range(DEPTH):
        j = li // 2
        if li % 2 == 0:
            past = None if paged is None else tuple(gather_pages(c[j], page_table) for c in paged[:6])
            h, rows = ab_mixer(x, past, w_in_ab[j], b_forget[j], w_out_ab[j], rel_bias)
            rows_ab.append(rows)
        else:
            past = None if paged is None else tuple(gather_pages(c[j], page_table) for c in paged[6:])
            h, rows = sb_mixer(x, past, w_in_c[j], w_out_c[j])
            rows_c.append(rows)
        x = layer_norm(DN_ALPHA * x + h, ln_g[li, 0], ln_b[li, 0])
        x = layer_norm(DN_ALPHA * x + mem_cross_attend(x, w_xq[li], mem_k[li], mem_v[li], w_xo[li]),
                       ln_g[li, 1], ln_b[li, 1])
        x = layer_norm(DN_ALPHA * x + moe(x, w_router[li], b_router[li], w_gate_up[li], b_gate_up[li],
                                          w_down[li], b_down[li]), ln_g[li, 2], ln_b[li, 2])
    return x, rows_ab, rows_c


def stack_rows(rows):
    return tuple(jnp.stack([r[i] for r in rows]) for i in range(len(rows[0])))


def setup_inputs(seed: int = 0) -> dict:
    key = jax.random.key(seed)
    keys = iter(jax.random.split(key, 48))
    f32 = jnp.float32

    def nrm(shape, scale=1.0):
        return jax.random.normal(next(keys), shape, f32) * scale

    n_pages = PAST_LEN // PAGE_SIZE
    n_used = DEC_BATCH * n_pages
    n_pool = n_used + (n_used + 3) // 4
    page_table = jax.random.permutation(next(keys), n_pool)[:n_used].reshape(DEC_BATCH, n_pages).astype(jnp.int32)
    return {
        'x_prompt': nrm((BATCH, SEQ, D_MODEL)),
        'x_sample': nrm((DEC_BATCH, DEC_SEQ, D_MODEL)),
        'cache_a_k': nrm((N_EVEN, n_pool, PAGE_SIZE, KV_A, HEAD_DIM)),
        'cache_a_v': nrm((N_EVEN, n_pool, PAGE_SIZE, KV_A, HEAD_DIM)),
        'cache_a_idxk': nrm((N_EVEN, n_pool, PAGE_SIZE, IDX_DIM)),
        'cache_b_k': nrm((N_EVEN, n_pool, PAGE_SIZE, H_B, HEAD_DIM)),
        'cache_b_v': nrm((N_EVEN, n_pool, PAGE_SIZE, H_B, HEAD_DIM)),
        'cache_b_logf': jax.nn.log_sigmoid(FORGET_BIAS + nrm((N_EVEN, n_pool, PAGE_SIZE, H_B))),
        'cache_c_k': nrm((N_ODD, n_pool, PAGE_SIZE, H_C, HEAD_DIM)),
        'cache_c_v': nrm((N_ODD, n_pool, PAGE_SIZE, H_C, HEAD_DIM)),
        'cache_mem_k': nrm((DEPTH, DEC_BATCH, N_MEM, XA_HEADS, XA_HEAD_DIM)),
        'cache_mem_v': nrm((DEPTH, DEC_BATCH, N_MEM, XA_HEADS, XA_HEAD_DIM)),
        'page_table': page_table,
        'mem_prompt': nrm((BATCH, N_MEM, D_MODEL)),
        'w_in_ab': nrm((N_EVEN, D_MODEL, D_AB), D_MODEL ** -0.5),
        'b_forget': FORGET_BIAS + nrm((N_EVEN, H_B), 0.1),
        'w_out_ab': nrm((N_EVEN, (H_A + H_B) * HEAD_DIM, D_MODEL), DN_BETA * ((H_A + H_B) * HEAD_DIM) ** -0.5),
        'w_in_c': nrm((N_ODD, D_MODEL, 3 * H_C * HEAD_DIM), D_MODEL ** -0.5),
        'w_out_c': nrm((N_ODD, H_C * HEAD_DIM, D_MODEL), DN_BETA * (H_C * HEAD_DIM) ** -0.5),
        'rel_bias': nrm((N_BUCKETS, H_A), 0.5),
        'w_xq': nrm((DEPTH, D_MODEL, XA_HEADS * XA_HEAD_DIM), D_MODEL ** -0.5),
        'w_xkv': nrm((DEPTH, D_MODEL, 2 * XA_HEADS * XA_HEAD_DIM), D_MODEL ** -0.5),
        'w_xo': nrm((DEPTH, XA_HEADS * XA_HEAD_DIM, D_MODEL), DN_BETA * (XA_HEADS * XA_HEAD_DIM) ** -0.5),
        'ln_g': 1.0 + nrm((DEPTH, 3, D_MODEL), 0.05),
        'ln_b': nrm((DEPTH, 3, D_MODEL), 0.05),
        'w_router': nrm((DEPTH, D_MODEL, N_EXPERTS), D_MODEL ** -0.5),
        'b_router': nrm((DEPTH, N_EXPERTS), 0.01),
        'w_gate_up': nrm((DEPTH, N_EXPERTS, D_MODEL, 2 * D_EXPERT), D_MODEL ** -0.5),
        'b_gate_up': nrm((DEPTH, N_EXPERTS, 2 * D_EXPERT), 0.01),
        'w_down': nrm((DEPTH, N_EXPERTS, D_EXPERT, D_MODEL), DN_BETA * D_EXPERT ** -0.5),
        'b_down': nrm((DEPTH, N_EXPERTS, D_MODEL), 0.01),
    }


def reference(x_prompt, x_sample, cache_a_k, cache_a_v, cache_a_idxk, cache_b_k, cache_b_v, cache_b_logf,
              cache_c_k, cache_c_v, cache_mem_k, cache_mem_v, page_table, mem_prompt,
              w_in_ab, b_forget, w_out_ab, w_in_c, w_out_c, rel_bias, w_xq, w_xkv, w_xo, ln_g, ln_b,
              w_router, b_router, w_gate_up, b_gate_up, w_down, b_down):
    params = (w_in_ab, b_forget, w_out_ab, w_in_c, w_out_c, rel_bias, w_xq, w_xo, ln_g, ln_b,
              w_router, b_router, w_gate_up, b_gate_up, w_down, b_down)
    mem_kv = [mem_project(mem_prompt, w_xkv[li]) for li in range(DEPTH)]
    p_mem_k = jnp.stack([kv[0] for kv in mem_kv])
    p_mem_v = jnp.stack([kv[1] for kv in mem_kv])
    y_prompt, rows_ab_p, rows_c_p = run_trunk(x_prompt, p_mem_k, p_mem_v, None, page_table, *params)
    paged = (cache_a_k, cache_a_v, cache_a_idxk, cache_b_k, cache_b_v, cache_b_logf, cache_c_k, cache_c_v)
    y_sample, rows_ab_s, rows_c_s = run_trunk(x_sample, cache_mem_k, cache_mem_v, paged, page_table, *params)
    p_a_k, p_a_v, p_a_idxk, p_b_k, p_b_v, p_b_logf = stack_rows(rows_ab_p)
    p_c_k, p_c_v = stack_rows(rows_c_p)
    s_a_k, s_a_v, s_a_idxk, s_b_k, s_b_v, s_b_logf = stack_rows(rows_ab_s)
    s_c_k, s_c_v = stack_rows(rows_c_s)
    return (y_prompt, y_sample,
            p_a_k, p_a_v, p_a_idxk, p_b_k, p_b_v, p_b_logf, p_c_k, p_c_v, p_mem_k, p_mem_v,
            s_a_k, s_a_v, s_a_idxk, s_b_k, s_b_v, s_b_logf, s_c_k, s_c_v)
```

```python
import functools
import math

import jax
import jax.numpy as jnp
from jax import lax
from jax.experimental import pallas as pl
from jax.experimental.pallas import tpu as pltpu

F32 = jnp.float32
BF16 = jnp.bfloat16
I32 = jnp.int32

D_MODEL = 1024
DEPTH = 4
PAGE_SIZE = 128
HEAD_DIM = 64
H_A = 8
KV_A = 2
IDX_HEADS = 8
IDX_DIM = 64
TOPK_MAX = 256
H_B = 8
H_C = 16
N_BUCKETS = 32
MAX_DISTANCE = 128
XA_HEADS = 4
XA_HEAD_DIM = 128
N_EXPERTS = 32
TOP_K = 4
D_EXPERT = D_MODEL
SWIGLU_LIMIT = 7.0
SWIGLU_ALPHA = 1.702
LN_EPS = 1e-5
NEG_INF = -1e30
ATTN_SCALE = HEAD_DIM ** -0.5
DN_ALPHA = (2 * DEPTH) ** 0.25

LANES = 128
Q_TILE = 128
ROW_TILE = 384
MOE_TILE = 256
VMEM_LIMIT = 56 * 1024 * 1024

MISC_IK = 0
MISC_IW = IDX_DIM
MISC_FL = IDX_DIM + IDX_HEADS

_HI = lax.Precision.HIGHEST


def _cparams(sem):
    return pltpu.CompilerParams(dimension_semantics=sem, vmem_limit_bytes=VMEM_LIMIT)


def _log_sigmoid(x):
    return -(jnp.maximum(-x, 0.0) + jnp.log1p(jnp.exp(-jnp.abs(x))))


def _nt_dot(a, b):
    return lax.dot_general(a, b, (((1,), (1,)), ((), ())), preferred_element_type=F32)


def _linear_kernel(x_ref, w_ref, *o_refs):
    xb = x_ref[...].astype(BF16)
    off = 0
    for o in o_refs:
        n = o.shape[-1]
        o[...] = jnp.dot(xb, w_ref[:, off:off + n], preferred_element_type=F32)
        off += n


def linear(x, w_bf16, splits, tm=ROW_TILE, name="linear"):
    m, k = x.shape
    n = w_bf16.shape[1]
    assert sum(splits) == n and all(s % LANES == 0 for s in splits) and m % tm == 0
    return pl.pallas_call(
        _linear_kernel,
        grid=(m // tm,),
        in_specs=[pl.BlockSpec((tm, k), lambda i: (i, 0)),
                  pl.BlockSpec((k, n), lambda i: (0, 0))],
        out_specs=[pl.BlockSpec((tm, s), lambda i: (i, 0)) for s in splits],
        out_shape=[jax.ShapeDtypeStruct((m, s), F32) for s in splits],
        compiler_params=_cparams(("parallel",)),
        name=name,
    )(x, w_bf16)


def _ab_proj_kernel(x_ref, w_ref, bf_ref, qa_ref, kva_ref, iq_ref, misc_ref, qb_ref, kb_ref, vb_ref):
    xb = x_ref[...].astype(BF16)
    off = 0
    for o in (qa_ref, kva_ref, iq_ref, misc_ref, qb_ref, kb_ref, vb_ref):
        n = o.shape[-1]
        r = jnp.dot(xb, w_ref[:, off:off + n], preferred_element_type=F32)
        if o is misc_ref:
            lane = lax.broadcasted_iota(I32, r.shape, 1)
            is_fl = (lane >= MISC_FL) & (lane < MISC_FL + H_B)
            r = jnp.where(is_fl, _log_sigmoid(r + bf_ref[...]), r)
        o[...] = r
        off += n


def ab_proj(x, w_bf16, bf_row, tm=ROW_TILE):
    m, k = x.shape
    splits = (H_A * HEAD_DIM, 2 * KV_A * HEAD_DIM, IDX_HEADS * IDX_DIM, LANES,
              H_B * HEAD_DIM, H_B * HEAD_DIM, H_B * HEAD_DIM)
    n = sum(splits)
    return pl.pallas_call(
        _ab_proj_kernel,
        grid=(m // tm,),
        in_specs=[pl.BlockSpec((tm, k), lambda i: (i, 0)),
                  pl.BlockSpec((k, n), lambda i: (0, 0)),
                  pl.BlockSpec((1, LANES), lambda i: (0, 0))],
        out_specs=[pl.BlockSpec((tm, s), lambda i: (i, 0)) for s in splits],
        out_shape=[jax.ShapeDtypeStruct((m, s), F32) for s in splits],
        compiler_params=_cparams(("parallel",)),
        name="ab_proj",
    )(x, w_bf16, bf_row)


def _layer_norm(z, g, b):
    mu = jnp.mean(z, axis=-1, keepdims=True)
    zc = z - mu
    var = jnp.mean(zc * zc, axis=-1, keepdims=True)
    return zc * lax.rsqrt(var + LN_EPS) * g + b


def _linear_res_ln_kernel(h_ref, w_ref, x_ref, g_ref, b_ref, o_ref):
    y = jnp.dot(h_ref[...].astype(BF16), w_ref[...], preferred_element_type=F32)
    o_ref[...] = _layer_norm(DN_ALPHA * x_ref[...] + y, g_ref[...], b_ref[...])


def linear_res_ln(h, w_bf16, x, g, b, tm=ROW_TILE, name="linear_res_ln"):
    m, k = h.shape
    d = w_bf16.shape[1]
    return pl.pallas_call(
        _linear_res_ln_kernel,
        grid=(m // tm,),
        in_specs=[pl.BlockSpec((tm, k), lambda i: (i, 0)),
                  pl.BlockSpec((k, d), lambda i: (0, 0)),
                  pl.BlockSpec((tm, d), lambda i: (i, 0)),
                  pl.BlockSpec((1, d), lambda i: (0, 0)),
                  pl.BlockSpec((1, d), lambda i: (0, 0))],
        out_specs=pl.BlockSpec((tm, d), lambda i: (i, 0)),
        out_shape=jax.ShapeDtypeStruct((m, d), F32),
        compiler_params=_cparams(("parallel",)),
        name=name,
    )(h, w_bf16, x, g.reshape(1, d), b.reshape(1, d))


def _linear2_res_ln_kernel(h1_ref, w1_ref, h2_ref, w2_ref, x_ref, g_ref, b_ref, o_ref):
    y = jnp.dot(h1_ref[...].astype(BF16), w1_ref[...], preferred_element_type=F32)
    y = y + jnp.dot(h2_ref[...].astype(BF16), w2_ref[...], preferred_element_type=F32)
    o_ref[...] = _layer_norm(DN_ALPHA * x_ref[...] + y, g_ref[...], b_ref[...])


def linear2_res_ln(h1, w1, h2, w2, x, g, b, tm=ROW_TILE):
    m, k1 = h1.shape
    k2 = h2.shape[1]
    d = w1.shape[1]
    return pl.pallas_call(
        _linear2_res_ln_kernel,
        grid=(m // tm,),
        in_specs=[pl.BlockSpec((tm, k1), lambda i: (i, 0)),
                  pl.BlockSpec((k1, d), lambda i: (0, 0)),
                  pl.BlockSpec((tm, k2), lambda i: (i, 0)),
                  pl.BlockSpec((k2, d), lambda i: (0, 0)),
                  pl.BlockSpec((tm, d), lambda i: (i, 0)),
                  pl.BlockSpec((1, d), lambda i: (0, 0)),
                  pl.BlockSpec((1, d), lambda i: (0, 0))],
        out_specs=pl.BlockSpec((tm, d), lambda i: (i, 0)),
        out_shape=jax.ShapeDtypeStruct((m, d), F32),
        compiler_params=_cparams(("parallel",)),
        name="ab_out",
    )(h1, w1, h2, w2, x, g.reshape(1, d), b.reshape(1, d))


def _res_ln_kernel(y_ref, x_ref, g_ref, b_ref, o_ref):
    o_ref[...] = _layer_norm(DN_ALPHA * x_ref[...] + y_ref[...], g_ref[...], b_ref[...])


def res_ln(y, x, g, b, tm=ROW_TILE):
    m, d = x.shape
    return pl.pallas_call(
        _res_ln_kernel,
        grid=(m // tm,),
        in_specs=[pl.BlockSpec((tm, d), lambda i: (i, 0)),
                  pl.BlockSpec((tm, d), lambda i: (i, 0)),
                  pl.BlockSpec((1, d), lambda i: (0, 0)),
                  pl.BlockSpec((1, d), lambda i: (0, 0))],
        out_specs=pl.BlockSpec((tm, d), lambda i: (i, 0)),
        out_shape=jax.ShapeDtypeStruct((m, d), F32),
        compiler_params=_cparams(("parallel",)),
        name="res_ln",
    )(y, x, g.reshape(1, d), b.reshape(1, d))


def _t5_bucket(rel):
    n = jnp.maximum(rel, 0)
    max_exact = N_BUCKETS // 2
    nf = jnp.maximum(n, 1).astype(F32)
    large = max_exact + (jnp.log(nf / max_exact) / math.log(MAX_DISTANCE / max_exact)
                         * (N_BUCKETS - max_exact)).astype(I32)
    large = jnp.minimum(large, N_BUCKETS - 1)
    return jnp.where(n < max_exact, n, large)


def _bias_lookup(bucket, tab_col):
    out = jnp.zeros(bucket.shape, F32)
    for b in range(N_BUCKETS):
        out = jnp.where(bucket == b, tab_col[:, b:b + 1], out)
    return out


def _bias_window_kernel(tab_ref, o_ref):
    r = lax.broadcasted_iota(I32, (Q_TILE, 2 * Q_TILE), 0)
    c = lax.broadcasted_iota(I32, (Q_TILE, 2 * Q_TILE), 1)
    bucket = _t5_bucket(r + Q_TILE - c)
    tab_col = jnp.broadcast_to(tab_ref[0], (Q_TILE, N_BUCKETS))
    o_ref[0] = _bias_lookup(bucket, tab_col)


def bias_window(rel_bias):
    tab_t = rel_bias.T.reshape(H_A, 1, N_BUCKETS)
    return pl.pallas_call(
        _bias_window_kernel,
        grid=(H_A,),
        in_specs=[pl.BlockSpec((1, 1, N_BUCKETS), lambda h: (h, 0, 0))],
        out_specs=pl.BlockSpec((1, Q_TILE, 2 * Q_TILE), lambda h: (h, 0, 0)),
        out_shape=jax.ShapeDtypeStruct((H_A, Q_TILE, 2 * Q_TILE), F32),
        compiler_params=_cparams(("parallel",)),
        name="bias_window",
    )(tab_t)


def _bias_rows_kernel(tab_ref, o_ref, *, past, n_q):
    rows, cols = o_ref.shape
    t = lax.broadcasted_iota(I32, (rows, cols), 0) // H_A
    s = lax.broadcasted_iota(I32, (rows, cols), 1)
    o_ref[...] = _bias_lookup(_t5_bucket(past + t - s), tab_ref[...])


def bias_rows(rel_bias, past, n_q, n_cols):
    tab_rows = jnp.tile(rel_bias.T, (n_q, 1))
    return pl.pallas_call(
        functools.partial(_bias_rows_kernel, past=past, n_q=n_q),
        out_shape=jax.ShapeDtypeStruct((n_q * H_A, n_cols), F32),
        compiler_params=pltpu.CompilerParams(vmem_limit_bytes=VMEM_LIMIT),
        name="bias_rows",
    )(tab_rows)


def _sort_key(score):
    bits = pltpu.bitcast(score + 0.0, I32)
    return bits ^ ((bits >> 31) & 0x7FFFFFFF)


def _count(mask):
    return jnp.sum(jnp.where(mask, 1.0, 0.0), axis=-1, keepdims=True)


def _topk_threshold(key, k, cut_ref):
    r, n = key.shape
    kf = float(k)

    def body(it, ans):
        cand = ans + jnp.left_shift(jnp.int32(1), 31 - it)
        cnt = _count(key >= cand)
        return jnp.where(cnt >= kf, cand, ans)

    thr = lax.fori_loop(0, 32, body, jnp.full((r, 1), -2 ** 31, I32))
    eq = key == thr
    need = kf - _count(key > thr)
    n_eq = _count(eq)
    cut_ref[...] = jnp.full((r, 1), n, I32)

    @pl.when(jnp.max(n_eq - need) > 0.0)
    def _():
        idx = lax.broadcasted_iota(I32, (r, n), 1)
        n_bits = max(1, (n - 1).bit_length())

        def cbody(it, c):
            cand = c + jnp.left_shift(jnp.int32(1), n_bits - 1 - it)
            cnt = _count(eq & (idx < cand))
            return jnp.where(cnt < need, cand, c)

        cut_ref[...] = lax.fori_loop(0, n_bits, cbody, jnp.zeros((r, 1), I32))

    return thr


def _dsa_prompt_kernel(far_ref, qa_ref, iq_ref, mq_ref, kva_ref, mk_ref, win_ref, o_ref,
                       s_ref, cut_ref, *, n_sel):
    i = pl.program_id(1)
    tq, t_keys = s_ref.shape
    qpos = i * tq + lax.broadcasted_iota(I32, (tq, t_keys), 0)
    kpos = lax.broadcasted_iota(I32, (tq, t_keys), 1)
    causal = kpos <= qpos

    ikb = mk_ref[:, MISC_IK:MISC_IK + IDX_DIM].astype(BF16)
    score = jnp.zeros((tq, t_keys), F32)
    for n in range(IDX_HEADS):
        iqn = iq_ref[:, n * IDX_DIM:(n + 1) * IDX_DIM].astype(BF16)
        dots = _nt_dot(iqn, ikb)
        score = score + jnp.maximum(dots, 0.0) * mq_ref[:, MISC_IW + n:MISC_IW + n + 1]
    score = jnp.where(causal, score, -jnp.inf)
    key = _sort_key(score)
    thr = _topk_threshold(key, n_sel, cut_ref)
    sel = ((key > thr) | ((key == thr) & (kpos <= cut_ref[...]))) & causal

    off = pl.multiple_of(jnp.maximum(i - 1, 0) * tq, tq)
    rep = H_A // KV_A
    for h in range(H_A):
        g = h // rep
        qh = qa_ref[:, h * HEAD_DIM:(h + 1) * HEAD_DIM].astype(BF16)
        kg = kva_ref[:, g * HEAD_DIM:(g + 1) * HEAD_DIM].astype(BF16)
        vg = kva_ref[:, (KV_A + g) * HEAD_DIM:(KV_A + g + 1) * HEAD_DIM].astype(BF16)
        far = far_ref[h]
        s_ref[...] = _nt_dot(qh, kg) * ATTN_SCALE + far
        w = win_ref[h] - far
        w = jnp.where(i == 0, jnp.concatenate([w[:, tq:], w[:, :tq]], axis=1), w)
        s_ref[:, pl.ds(off, 2 * tq)] += w
        logits = jnp.where(sel, s_ref[...], NEG_INF)
        m = jnp.max(logits, axis=-1, keepdims=True)
        p = jnp.exp(logits - m)
        p = p / jnp.sum(p, axis=-1, keepdims=True)
        o_ref[:, h * HEAD_DIM:(h + 1) * HEAD_DIM] = jnp.dot(p.astype(BF16), vg, preferred_element_type=F32)


def dsa_prompt(qa, iq, misc, kva, win, far, n_batch, t):
    nq = t // Q_TILE
    n_sel = min(TOPK_MAX, t // 4)
    grid_spec = pltpu.PrefetchScalarGridSpec(
        num_scalar_prefetch=1,
        grid=(n_batch, nq),
        in_specs=[pl.BlockSpec((Q_TILE, H_A * HEAD_DIM), lambda b, i, far: (b * nq + i, 0)),
                  pl.BlockSpec((Q_TILE, IDX_HEADS * IDX_DIM), lambda b, i, far: (b * nq + i, 0)),
                  pl.BlockSpec((Q_TILE, LANES), lambda b, i, far: (b * nq + i, 0)),
                  pl.BlockSpec((t, 2 * KV_A * HEAD_DIM), lambda b, i, far: (b, 0)),
                  pl.BlockSpec((t, LANES), lambda b, i, far: (b, 0)),
                  pl.BlockSpec((H_A, Q_TILE, 2 * Q_TILE), lambda b, i, far: (0, 0, 0))],
        out_specs=pl.BlockSpec((Q_TILE, H_A * HEAD_DIM), lambda b, i, far: (b * nq + i, 0)),
        scratch_shapes=[pltpu.VMEM((Q_TILE, t), F32), pltpu.VMEM((Q_TILE, 1), I32)],
    )
    return pl.pallas_call(
        functools.partial(_dsa_prompt_kernel, n_sel=n_sel),
        grid_spec=grid_spec,
        out_shape=jax.ShapeDtypeStruct((n_batch * t, H_A * HEAD_DIM), F32),
        compiler_params=_cparams(("parallel", "parallel")),
        name="dsa_prompt",
    )(far, qa, iq, misc, kva, misc, win)


def _fox_prompt_kernel(q_ref, k_ref, v_ref, cum_ref, cumt_ref, o_ref):
    i = pl.program_id(2)
    hp = pl.program_id(1)
    tq = q_ref.shape[0]
    t_keys = k_ref.shape[0]
    qpos = i * tq + lax.broadcasted_iota(I32, (tq, t_keys), 0)
    kpos = lax.broadcasted_iota(I32, (tq, t_keys), 1)
    causal = kpos <= qpos
    lane = lax.broadcasted_iota(I32, cum_ref.shape, 1)
    for hh in range(2):
        sl = slice(hh * HEAD_DIM, (hh + 1) * HEAD_DIM)
        head = 2 * hp + hh
        qh = q_ref[:, sl].astype(BF16)
        kh = k_ref[:, sl].astype(BF16)
        vh = v_ref[:, sl].astype(BF16)
        cq = jnp.sum(jnp.where(lane == MISC_FL + head, cum_ref[...], 0.0), axis=-1, keepdims=True)
        ck = cumt_ref[0, pl.ds(head, 1), :]
        logits = _nt_dot(qh, kh) * ATTN_SCALE
        logits = logits + cq - ck
        logits = jnp.where(causal, logits, NEG_INF)
        m = jnp.max(logits, axis=-1, keepdims=True)
        p = jnp.exp(logits - m)
        p = p / jnp.sum(p, axis=-1, keepdims=True)
        o_ref[:, sl] = jnp.dot(p.astype(BF16), vh, preferred_element_type=F32)


def fox_prompt(qb, kb, vb, cum, cumt, n_batch, t, tq=2 * Q_TILE):
    assert MISC_FL % 8 == 0
    nq = t // tq
    hw = 2 * HEAD_DIM
    return pl.pallas_call(
        _fox_prompt_kernel,
        grid=(n_batch, H_B // 2, nq),
        in_specs=[pl.BlockSpec((tq, hw), lambda b, h, i: (b * nq + i, h)),
                  pl.BlockSpec((t, hw), lambda b, h, i: (b, h)),
                  pl.BlockSpec((t, hw), lambda b, h, i: (b, h)),
                  pl.BlockSpec((tq, LANES), lambda b, h, i: (b * nq + i, 0)),
                  pl.BlockSpec((1, H_B, t), lambda b, h, i: (b, MISC_FL // H_B, 0))],
        out_specs=pl.BlockSpec((tq, hw), lambda b, h, i: (b * nq + i, h)),
        out_shape=jax.ShapeDtypeStruct((n_batch * t, H_B * HEAD_DIM), F32),
        compiler_params=_cparams(("parallel", "parallel", "parallel")),
        name="fox_prompt",
    )(qb, kb, vb, cum, cumt)


def _split_suffix(lk, tri2):
    hi = lk.astype(BF16)
    lo = (lk - hi.astype(F32)).astype(BF16)
    return jnp.dot(jnp.concatenate([hi, lo], axis=1), tri2, preferred_element_type=F32)


def _strict_lower_twice(n):
    j = lax.broadcasted_iota(I32, (2 * n, n), 0)
    s = lax.broadcasted_iota(I32, (2 * n, n), 1)
    j = jnp.where(j >= n, j - n, j)
    return jnp.where(j > s, 1.0, 0.0).astype(BF16)


def _sb_prompt_kernel(q_ref, k_ref, v_ref, o_ref):
    i = pl.program_id(2)
    tq = q_ref.shape[0]
    tri2 = _strict_lower_twice(tq)
    row = lax.broadcasted_iota(I32, (tq, tq), 0)
    col = lax.broadcasted_iota(I32, (tq, tq), 1)
    strict = col < row
    qs = [q_ref[:, hh * HEAD_DIM:(hh + 1) * HEAD_DIM].astype(BF16) for hh in range(2)]

    def tile(j, hh, carry, acc, mask):
        base = pl.multiple_of(j * tq, tq)
        kh = k_ref[pl.ds(base, tq), hh * HEAD_DIM:(hh + 1) * HEAD_DIM].astype(BF16)
        vh = v_ref[pl.ds(base, tq), hh * HEAD_DIM:(hh + 1) * HEAD_DIM].astype(BF16)
        z = _nt_dot(qs[hh], kh) * ATTN_SCALE
        soft = jnp.log1p(jnp.exp(-jnp.abs(z)))
        log_keep = -(jnp.maximum(z, 0.0) + soft)
        log_hit = -(jnp.maximum(-z, 0.0) + soft)
        if mask is not None:
            log_keep = jnp.where(mask, log_keep, 0.0)
        after = _split_suffix(log_keep, tri2) + carry
        w = jnp.exp(log_hit + after)
        if mask is not None:
            w = jnp.where(mask, w, 0.0)
        acc = acc + jnp.dot(w.astype(BF16), vh, preferred_element_type=F32)
        carry = carry + jnp.sum(log_keep, axis=-1, keepdims=True)
        return carry, acc

    state = []
    for hh in range(2):
        state.extend(tile(i, hh, jnp.zeros((tq, 1), F32), jnp.zeros((tq, HEAD_DIM), F32), strict))

    def body(step, st):
        j = i - 1 - step
        c0, a0 = tile(j, 0, st[0], st[1], None)
        c1, a1 = tile(j, 1, st[2], st[3], None)
        return (c0, a0, c1, a1)

    st = lax.fori_loop(0, i, body, tuple(state))
    o_ref[:, :HEAD_DIM] = st[1]
    o_ref[:, HEAD_DIM:] = st[3]


def sb_prompt(q, k, v, n_batch, t, col0=(0, 0, 0)):
    nq = t // Q_TILE
    hw = 2 * HEAD_DIM
    cq, ck, cv = col0
    return pl.pallas_call(
        _sb_prompt_kernel,
        grid=(n_batch, H_C // 2, nq),
        in_specs=[pl.BlockSpec((Q_TILE, hw), lambda b, h, i: (b * nq + i, cq + h)),
                  pl.BlockSpec((t, hw), lambda b, h, i: (b, ck + h)),
                  pl.BlockSpec((t, hw), lambda b, h, i: (b, cv + h))],
        out_specs=pl.BlockSpec((Q_TILE, hw), lambda b, h, i: (b * nq + i, h)),
        out_shape=jax.ShapeDtypeStruct((n_batch * t, H_C * HEAD_DIM), F32),
        compiler_params=_cparams(("parallel", "parallel", "parallel")),
        name="sb_prompt",
    )(q, k, v)


def _cumsum_prompt_kernel(m_ref, cum_ref, cumt_ref, carry_ref):
    i = pl.program_id(1)
    tb = m_ref.shape[0]

    @pl.when(i == 0)
    def _():
        carry_ref[...] = jnp.zeros_like(carry_ref)

    r = lax.broadcasted_iota(I32, (tb, tb), 0)
    c = lax.broadcasted_iota(I32, (tb, tb), 1)
    lower = jnp.where(c <= r, 1.0, 0.0)
    cum = jnp.dot(lower, m_ref[...], precision=_HI, preferred_element_type=F32) + carry_ref[...]
    cum_ref[...] = cum
    carry_ref[...] = cum[tb - 1:tb, :]
    eye = jnp.where(lax.broadcasted_iota(I32, (LANES, LANES), 0) == lax.broadcasted_iota(I32, (LANES, LANES), 1),
                    1.0, 0.0)
    cumt_ref[0] = lax.dot_general(eye, cum, (((1,), (1,)), ((), ())), precision=_HI,
                                  preferred_element_type=F32)


def cumsum_prompt(misc, n_batch, t, tb=256):
    nb = t // tb
    return pl.pallas_call(
        _cumsum_prompt_kernel,
        grid=(n_batch, nb),
        in_specs=[pl.BlockSpec((tb, LANES), lambda b, i: (b * nb + i, 0))],
        out_specs=[pl.BlockSpec((tb, LANES), lambda b, i: (b * nb + i, 0)),
                   pl.BlockSpec((1, LANES, tb), lambda b, i: (b, 0, i))],
        out_shape=[jax.ShapeDtypeStruct((n_batch * t, LANES), F32),
                   jax.ShapeDtypeStruct((n_batch, LANES, t), F32)],
        scratch_shapes=[pltpu.VMEM((1, LANES), F32)],
        compiler_params=_cparams(("parallel", "arbitrary")),
        name="cumsum_prompt",
    )(misc)


def _cumsum_pages_kernel(pt_ref, *refs, n_group):
    page_refs = refs[:n_group]
    new_ref, cum_ref, cumnew_ref, carry_ref = refs[n_group:]
    p = pl.program_id(1)

    @pl.when(p == 0)
    def _():
        carry_ref[...] = jnp.zeros_like(carry_ref)

    eye = jnp.where(lax.broadcasted_iota(I32, (H_B, H_B), 0) == lax.broadcasted_iota(I32, (H_B, H_B), 1), 1.0, 0.0)
    r = lax.broadcasted_iota(I32, (PAGE_SIZE, PAGE_SIZE), 0)
    c = lax.broadcasted_iota(I32, (PAGE_SIZE, PAGE_SIZE), 1)
    upper = jnp.where(r <= c, 1.0, 0.0)
    carry = carry_ref[...]
    for g in range(n_group):
        lft = lax.dot_general(eye, page_refs[g][...], (((1,), (1,)), ((), ())), precision=_HI,
                              preferred_element_type=F32)
        cum = jnp.dot(lft, upper, precision=_HI, preferred_element_type=F32) + carry
        cum_ref[0, :, g * PAGE_SIZE:(g + 1) * PAGE_SIZE] = cum
        carry = cum[:, PAGE_SIZE - 1:PAGE_SIZE]
    carry_ref[...] = carry

    @pl.when(p == pl.num_programs(1) - 1)
    def _():
        n_new = new_ref.shape[1]
        newt = lax.dot_general(eye, new_ref[0], (((1,), (1,)), ((), ())), precision=_HI,
                               preferred_element_type=F32)
        lane = lax.broadcasted_iota(I32, (H_B, LANES), 1)
        out = jnp.zeros((H_B, LANES), F32)
        run = carry
        for s in range(n_new):
            run = run + newt[:, s:s + 1]
            out = jnp.where(lane == s, run, out)
        cumnew_ref[0] = out


def cumsum_pages(cache_logf, layer, page_table, logf_new, n_group=8):
    n_b, n_pages = page_table.shape
    steps = n_pages // n_group

    def page_map(g):
        return lambda b, p, pt: (layer, pt[b, p * n_group + g], 0, 0)

    grid_spec = pltpu.PrefetchScalarGridSpec(
        num_scalar_prefetch=1,
        grid=(n_b, steps),
        in_specs=[pl.BlockSpec((None, None, PAGE_SIZE, H_B), page_map(g)) for g in range(n_group)]
        + [pl.BlockSpec((1, logf_new.shape[1], H_B), lambda b, p, pt: (b, 0, 0))],
        out_specs=[pl.BlockSpec((1, H_B, n_group * PAGE_SIZE), lambda b, p, pt: (b, 0, p)),
                   pl.BlockSpec((1, H_B, LANES), lambda b, p, pt: (b, 0, 0))],
        scratch_shapes=[pltpu.VMEM((H_B, 1), F32)],
    )
    return pl.pallas_call(
        functools.partial(_cumsum_pages_kernel, n_group=n_group),
        grid_spec=grid_spec,
        out_shape=[jax.ShapeDtypeStruct((n_b, H_B, n_pages * PAGE_SIZE), F32),
                   jax.ShapeDtypeStruct((n_b, H_B, LANES), F32)],
        compiler_params=_cparams(("parallel", "arbitrary")),
        name="cumsum_pages",
    )(page_table, *([cache_logf] * n_group), logf_new)


def _xattn_kernel(q_ref, k_ref, v_ref, *rest, rows):
    o_ref = rest[-1]
    scale = XA_HEAD_DIM ** -0.5
    for bi in range(k_ref.shape[0]):
        rs = slice(bi * rows, (bi + 1) * rows)
        for h in range(XA_HEADS):
            cs = slice(h * XA_HEAD_DIM, (h + 1) * XA_HEAD_DIM)
            q = q_ref[rs, cs].astype(BF16)
            k = k_ref[bi, :, cs].astype(BF16)
            v = v_ref[bi, :, cs].astype(BF16)
            logits = _nt_dot(q, k) * scale
            m = jnp.max(logits, axis=-1, keepdims=True)
            p = jnp.exp(logits - m)
            p = p / jnp.sum(p, axis=-1, keepdims=True)
            o_ref[rs, cs] = jnp.dot(p.astype(BF16), v, preferred_element_type=F32)


def xattn(q, mem_k, mem_v, row0, rows, n_bblk, prev=None):
    n_tok, dq = q.shape
    n_b, n_mem, _ = mem_k.shape
    if rows >= 8:
        tq = min(rows, 512)
        per = rows // tq
        steps = n_b * per
        q_spec = pl.BlockSpec((tq, dq), lambda i: (row0 // tq + i, 0))
        m_spec = pl.BlockSpec((1, n_mem, dq), lambda i: (i // per, 0, 0))
        kern_rows = tq
    else:
        tq = rows * n_bblk
        steps = n_b // n_bblk
        q_spec = pl.BlockSpec((tq, dq), lambda i: (row0 // tq + i, 0))
        m_spec = pl.BlockSpec((n_bblk, n_mem, dq), lambda i: (i, 0, 0))
        kern_rows = rows
    assert row0 % tq == 0
    in_specs = [q_spec, m_spec, m_spec]
    args = [q, mem_k, mem_v]
    aliases = {}
    if prev is not None:
        in_specs.append(pl.BlockSpec(memory_space=pl.ANY))
        args.append(prev)
        aliases = {3: 0}
    return pl.pallas_call(
        functools.partial(_xattn_kernel, rows=kern_rows),
        grid=(steps,),
        in_specs=in_specs,
        out_specs=q_spec,
        out_shape=jax.ShapeDtypeStruct((n_tok, dq), F32),
        input_output_aliases=aliases,
        compiler_params=_cparams(("parallel",)),
        name="xattn",
    )(*args)


def _router_kernel(x_ref, w_ref, b_ref, idx_ref, gate_ref):
    logits = jnp.dot(x_ref[...], w_ref[...], precision=_HI, preferred_element_type=F32) + b_ref[...]
    tm = logits.shape[0]
    lane = lax.broadcasted_iota(I32, logits.shape, 1)
    lane_k = lax.broadcasted_iota(I32, (tm, TOP_K), 1)
    vals = jnp.zeros((tm, TOP_K), F32)
    idxs = jnp.zeros((tm, TOP_K), I32)
    cur = logits
    for r in range(TOP_K):
        m = jnp.max(cur, axis=-1, keepdims=True)
        ix = jnp.min(jnp.where(cur == m, lane, N_EXPERTS), axis=-1, keepdims=True)
        vals = jnp.where(lane_k == r, m, vals)
        idxs = jnp.where(lane_k == r, ix, idxs)
        cur = jnp.where(lane == ix, -jnp.inf, cur)
    e = jnp.exp(vals - jnp.max(vals, axis=-1, keepdims=True))
    gate_ref[...] = e / jnp.sum(e, axis=-1, keepdims=True)
    idx_ref[...] = idxs


def router(x, w_router, b_router, tm=ROW_TILE):
    m, d = x.shape
    return pl.pallas_call(
        _router_kernel,
        grid=(m // tm,),
        in_specs=[pl.BlockSpec((tm, d), lambda i: (i, 0)),
                  pl.BlockSpec((d, N_EXPERTS), lambda i: (0, 0)),
                  pl.BlockSpec((1, N_EXPERTS), lambda i: (0, 0))],
        out_specs=[pl.BlockSpec((tm, TOP_K), lambda i: (i, 0)),
                   pl.BlockSpec((tm, TOP_K), lambda i: (i, 0))],
        out_shape=[jax.ShapeDtypeStruct((m, TOP_K), I32), jax.ShapeDtypeStruct((m, TOP_K), F32)],
        compiler_params=_cparams(("parallel",)),
        name="router",
    )(x, w_router, b_router.reshape(1, N_EXPERTS))


def _expert_kernel(bexp_ref, bval_ref, x_ref, gate_ref, wgu_ref, bgu_ref, wd_ref, bd_ref, o_ref,
                   wgu_s, wd_s):
    i = pl.program_id(0)
    prev = bexp_ref[jnp.maximum(i - 1, 0)]

    @pl.when((i == 0) | (bexp_ref[i] != prev))
    def _():
        wgu_s[...] = wgu_ref[0].astype(BF16)
        wd_s[...] = wd_ref[0].astype(BF16)

    @pl.when(bval_ref[i] > 0)
    def _():
        xb = x_ref[...].astype(BF16)
        de = wd_s.shape[0]
        g = jnp.dot(xb, wgu_s[:, :de], preferred_element_type=F32) + bgu_ref[0, :, :de]
        u = jnp.dot(xb, wgu_s[:, de:], preferred_element_type=F32) + bgu_ref[0, :, de:]
        g = jnp.minimum(g, SWIGLU_LIMIT)
        u = jnp.clip(u, -SWIGLU_LIMIT, SWIGLU_LIMIT)
        act = g * jax.nn.sigmoid(SWIGLU_ALPHA * g) * (u + 1.0)
        y = jnp.dot(act.astype(BF16), wd_s[...], preferred_element_type=F32) + bd_ref[0]
        o_ref[...] = y * gate_ref[...]

    @pl.when(bval_ref[i] == 0)
    def _():
        o_ref[...] = jnp.zeros_like(o_ref)


def expert_blocks(x_rows, row_gate, block_exp, block_valid, w_gu, b_gu, w_down, b_down, layer):
    n_rows, d = x_rows.shape
    n_blocks = n_rows // MOE_TILE
    de = w_down.shape[2]
    grid_spec = pltpu.PrefetchScalarGridSpec(
        num_scalar_prefetch=2,
        grid=(n_blocks,),
        in_specs=[pl.BlockSpec((MOE_TILE, d), lambda i, be, bv: (i, 0)),
                  pl.BlockSpec((MOE_TILE, 1), lambda i, be, bv: (i, 0)),
                  pl.BlockSpec((None, 1, d, 2 * de), lambda i, be, bv: (layer, be[i], 0, 0)),
                  pl.BlockSpec((None, 1, 1, 2 * de), lambda i, be, bv: (layer, be[i], 0, 0)),
                  pl.BlockSpec((None, 1, de, d), lambda i, be, bv: (layer, be[i], 0, 0)),
                  pl.BlockSpec((None, 1, 1, d), lambda i, be, bv: (layer, be[i], 0, 0))],
        out_specs=pl.BlockSpec((MOE_TILE, d), lambda i, be, bv: (i, 0)),
        scratch_shapes=[pltpu.VMEM((d, 2 * de), BF16), pltpu.VMEM((de, d), BF16)],
    )
    return pl.pallas_call(
        _expert_kernel,
        grid_spec=grid_spec,
        out_shape=jax.ShapeDtypeStruct((n_rows, d), F32),
        compiler_params=_cparams(("arbitrary",)),
        name="experts",
    )(block_exp, block_valid, x_rows, row_gate, w_gu, b_gu.reshape(*b_gu.shape[:2], 1, 2 * de),
      w_down, b_down.reshape(*b_down.shape[:2], 1, d))


def moe(x, w_router, b_router, w_gu, b_gu, w_down, b_down, layer):
    n_tok, d = x.shape
    top_idx, gate = router(x, w_router[layer], b_router[layer])
    flat_e = top_idx.reshape(-1)
    n_assign = n_tok * TOP_K
    order = jnp.argsort(flat_e).astype(I32)
    e_sorted = flat_e[order]
    counts = jnp.bincount(flat_e, length=N_EXPERTS).astype(I32)
    padded = (counts + MOE_TILE - 1) // MOE_TILE * MOE_TILE
    pad_end = jnp.cumsum(padded)
    pad_start = pad_end - padded
    start = jnp.cumsum(counts) - counts
    dest = pad_start[e_sorted] + jnp.arange(n_assign, dtype=I32) - start[e_sorted]
    n_blocks = -(-n_assign // MOE_TILE) + N_EXPERTS
    n_rows = n_blocks * MOE_TILE
    row_tok = jnp.full((n_rows,), n_tok, I32).at[dest].set(order // TOP_K)
    row_gate = jnp.zeros((n_rows,), F32).at[dest].set(gate.reshape(-1)[order])
    block_start = jnp.arange(n_blocks, dtype=I32) * MOE_TILE
    block_exp = jnp.minimum(jnp.searchsorted(pad_end, block_start, side='right'), N_EXPERTS - 1).astype(I32)
    block_valid = (block_start < pad_end[-1]).astype(I32)
    x_pad = jnp.concatenate([x, jnp.zeros((1, d), x.dtype)], axis=0)
    x_rows = x_pad[row_tok]
    y_rows = expert_blocks(x_rows, row_gate.reshape(n_rows, 1), block_exp, block_valid,
                           w_gu, b_gu, w_down, b_down, layer)
    y = jnp.zeros((n_tok + 1, d), F32).at[row_tok].add(y_rows)
    return y[:n_tok]


def _page_slab(ref):
    return jnp.concatenate([ref[:, h, :] for h in range(ref.shape[1])], axis=1)


def _fill_new_page(dst_ref, new_rows):
    dst_ref[...] = jnp.zeros_like(dst_ref)
    dst_ref[0:new_rows.shape[0], :] = new_rows


def _dsa_scores_kernel(pt_ref, *refs, n_group):
    page_refs = refs[:n_group]
    iq_ref, iw_ref, iknew_ref, sc_ref, scnew_ref, new_s = refs[n_group:]
    p = pl.program_id(1)
    iqm = iq_ref[0].astype(BF16)
    iw = iw_ref[0]
    n_q = iqm.shape[0] // IDX_HEADS

    def score_of(ik):
        dots = _nt_dot(iqm, ik.astype(BF16))
        w = jnp.maximum(dots, 0.0) * iw
        return jnp.sum(w.reshape(n_q, IDX_HEADS, ik.shape[0]), axis=1)

    for g in range(n_group):
        sc_ref[0, :, g * PAGE_SIZE:(g + 1) * PAGE_SIZE] = score_of(page_refs[g][...])

    @pl.when(p == 0)
    def _():
        n_new = iknew_ref.shape[1]
        _fill_new_page(new_s, iknew_ref[0])
        s = score_of(new_s[...])
        t = lax.broadcasted_iota(I32, s.shape, 0)
        c = lax.broadcasted_iota(I32, s.shape, 1)
        scnew_ref[0] = jnp.where((c <= t) & (c < n_new), s, -jnp.inf)


def dsa_scores_decode(cache_idxk, layer, page_table, iq_rows, iw_rows, ik_new, n_group=16):
    n_b, n_pages = page_table.shape
    n_q = iq_rows.shape[1] // IDX_HEADS
    steps = n_pages // n_group

    def page_map(g):
        return lambda b, p, pt: (layer, pt[b, p * n_group + g], 0, 0)

    grid_spec = pltpu.PrefetchScalarGridSpec(
        num_scalar_prefetch=1,
        grid=(n_b, steps),
        in_specs=[pl.BlockSpec((None, None, PAGE_SIZE, IDX_DIM), page_map(g)) for g in range(n_group)]
        + [pl.BlockSpec((1,) + iq_rows.shape[1:], lambda b, p, pt: (b, 0, 0)),
           pl.BlockSpec((1,) + iw_rows.shape[1:], lambda b, p, pt: (b, 0, 0)),
           pl.BlockSpec((1,) + ik_new.shape[1:], lambda b, p, pt: (b, 0, 0))],
        out_specs=[pl.BlockSpec((1, n_q, n_group * PAGE_SIZE), lambda b, p, pt: (b, 0, p)),
                   pl.BlockSpec((1, n_q, PAGE_SIZE), lambda b, p, pt: (b, 0, 0))],
        scratch_shapes=[pltpu.VMEM((PAGE_SIZE, IDX_DIM), F32)],
    )
    return pl.pallas_call(
        functools.partial(_dsa_scores_kernel, n_group=n_group),
        grid_spec=grid_spec,
        out_shape=[jax.ShapeDtypeStruct((n_b, n_q, n_pages * PAGE_SIZE), F32),
                   jax.ShapeDtypeStruct((n_b, n_q, PAGE_SIZE), F32)],
        compiler_params=_cparams(("parallel", "arbitrary")),
        name="dsa_scores_decode",
    )(page_table, *([cache_idxk] * n_group), iq_rows, iw_rows, ik_new)


def _select_kernel(s_ref, sel_ref, cut_ref, *, n_sel):
    key = _sort_key(s_ref[...])
    thr = _topk_threshold(key, n_sel, cut_ref)
    idx = lax.broadcasted_iota(I32, key.shape, 1)
    sel = (key > thr) | ((key == thr) & (idx <= cut_ref[...]))
    sel_ref[...] = jnp.where(sel, 1.0, 0.0)


def topk_select(scores, n_sel):
    return pl.pallas_call(
        functools.partial(_select_kernel, n_sel=n_sel),
        out_shape=jax.ShapeDtypeStruct(scores.shape, F32),
        scratch_shapes=[pltpu.VMEM((scores.shape[0], 1), I32)],
        compiler_params=pltpu.CompilerParams(vmem_limit_bytes=VMEM_LIMIT),
        name="topk_select",
    )(scores)


def _softmax_update(logits, mask, v_bf16, m_s, l_s, acc_s):
    if mask is not None:
        logits = jnp.where(mask, logits, NEG_INF)
    m_old = m_s[...]
    m_new = jnp.maximum(m_old, jnp.max(logits, axis=-1, keepdims=True))
    alpha = jnp.exp(m_old - m_new)
    pe = jnp.exp(logits - m_new)
    if mask is not None:
        pe = jnp.where(mask, pe, 0.0)
    l_s[...] = alpha * l_s[...] + jnp.sum(pe, axis=-1, keepdims=True)
    acc_s[...] = alpha * acc_s[...] + jnp.dot(pe.astype(BF16), v_bf16, preferred_element_type=F32)
    m_s[...] = m_new


def _expand_rows(x, reps):
    n_q, s = x.shape
    return jnp.broadcast_to(x[:, None, :], (n_q, reps, s)).reshape(n_q * reps, s)


def _dsa_attn_decode_kernel(pt_ref, *refs, n_group):
    k_refs = refs[:n_group]
    v_refs = refs[n_group:2 * n_group]
    (qbd_ref, kvnew_ref, sel_ref, selnew_ref, bias_ref, biasnew_ref, o_ref,
     m_s, l_s, acc_s, knew_s, vnew_s) = refs[2 * n_group:]
    p = pl.program_id(1)
    kw = KV_A * HEAD_DIM

    @pl.when(p == 0)
    def _():
        m_s[...] = jnp.full_like(m_s, NEG_INF)
        l_s[...] = jnp.zeros_like(l_s)
        acc_s[...] = jnp.zeros_like(acc_s)

    qbd = qbd_ref[0].astype(BF16)
    kcat = jnp.concatenate([_page_slab(r) for r in k_refs], axis=0).astype(BF16)
    vcat = jnp.concatenate([_page_slab(r) for r in v_refs], axis=0).astype(BF16)
    logits = _nt_dot(qbd, kcat) * ATTN_SCALE + bias_ref[...]
    mask = _expand_rows(sel_ref[0], H_A) > 0.5
    _softmax_update(logits, mask, vcat, m_s, l_s, acc_s)

    @pl.when(p == pl.num_programs(1) - 1)
    def _():
        n_new = kvnew_ref.shape[1]
        _fill_new_page(knew_s, kvnew_ref[0, :, :kw])
        _fill_new_page(vnew_s, kvnew_ref[0, :, kw:])
        lg = _nt_dot(qbd, knew_s[...].astype(BF16)) * ATTN_SCALE + biasnew_ref[...]
        t = lax.broadcasted_iota(I32, lg.shape, 0) // H_A
        c = lax.broadcasted_iota(I32, lg.shape, 1)
        mk = (_expand_rows(selnew_ref[0], H_A) > 0.5) & (c <= t) & (c < n_new)
        _softmax_update(lg, mk, vnew_s[...].astype(BF16), m_s, l_s, acc_s)
        o_ref[0] = acc_s[...] / l_s[...]


def dsa_attn_decode(cache_k, cache_v, layer, page_table, qbd, kv_new, sel, bias, n_group=8):
    n_b, n_pages = page_table.shape
    rows = qbd.shape[1]
    n_q = rows // H_A
    kw = KV_A * HEAD_DIM
    steps = n_pages // n_group
    gw = n_group * PAGE_SIZE

    def page_map(g):
        return lambda b, p, pt: (layer, pt[b, p * n_group + g], 0, 0, 0)

    page_spec = [pl.BlockSpec((None, None, PAGE_SIZE, KV_A, HEAD_DIM), page_map(g)) for g in range(n_group)]
    grid_spec = pltpu.PrefetchScalarGridSpec(
        num_scalar_prefetch=1,
        grid=(n_b, steps),
        in_specs=page_spec + page_spec
        + [pl.BlockSpec((1, rows, kw), lambda b, p, pt: (b, 0, 0)),
           pl.BlockSpec((1,) + kv_new.shape[1:], lambda b, p, pt: (b, 0, 0)),
           pl.BlockSpec((1, n_q, gw), lambda b, p, pt: (b, 0, p)),
           pl.BlockSpec((1, n_q, PAGE_SIZE), lambda b, p, pt: (b, 0, n_pages)),
           pl.BlockSpec((rows, gw), lambda b, p, pt: (0, p)),
           pl.BlockSpec((rows, PAGE_SIZE), lambda b, p, pt: (0, n_pages))],
        out_specs=pl.BlockSpec((1, rows, kw), lambda b, p, pt: (b, 0, 0)),
        scratch_shapes=[pltpu.VMEM((rows, 1), F32), pltpu.VMEM((rows, 1), F32), pltpu.VMEM((rows, kw), F32),
                        pltpu.VMEM((PAGE_SIZE, kw), F32), pltpu.VMEM((PAGE_SIZE, kw), F32)],
    )
    return pl.pallas_call(
        functools.partial(_dsa_attn_decode_kernel, n_group=n_group),
        grid_spec=grid_spec,
        out_shape=jax.ShapeDtypeStruct((n_b, rows, kw), F32),
        compiler_params=_cparams(("parallel", "arbitrary")),
        name="dsa_attn_decode",
    )(page_table, *([cache_k] * n_group), *([cache_v] * n_group), qbd, kv_new, sel, sel, bias, bias)


def _fox_attn_decode_kernel(pt_ref, *refs, n_group):
    k_refs = refs[:n_group]
    v_refs = refs[n_group:2 * n_group]
    (qbd_ref, knew_ref, vnew_ref, cq_ref, ck_ref, cknew_ref, o_ref,
     m_s, l_s, acc_s, knew_s, vnew_s) = refs[2 * n_group:]
    p = pl.program_id(1)
    n_q = qbd_ref.shape[1] // H_B

    @pl.when(p == 0)
    def _():
        m_s[...] = jnp.full_like(m_s, NEG_INF)
        l_s[...] = jnp.zeros_like(l_s)
        acc_s[...] = jnp.zeros_like(acc_s)

    qbd = qbd_ref[0].astype(BF16)
    cq = cq_ref[0]
    kcat = jnp.concatenate([_page_slab(r) for r in k_refs], axis=0).astype(BF16)
    vcat = jnp.concatenate([_page_slab(r) for r in v_refs], axis=0).astype(BF16)
    ck = jnp.concatenate([ck_ref[0]] * n_q, axis=0)
    logits = _nt_dot(qbd, kcat) * ATTN_SCALE + cq - ck
    _softmax_update(logits, None, vcat, m_s, l_s, acc_s)

    @pl.when(p == pl.num_programs(1) - 1)
    def _():
        n_new = knew_ref.shape[1]
        _fill_new_page(knew_s, knew_ref[0])
        _fill_new_page(vnew_s, vnew_ref[0])
        ckn = jnp.concatenate([cknew_ref[0]] * n_q, axis=0)
        lg = _nt_dot(qbd, knew_s[...].astype(BF16)) * ATTN_SCALE + cq - ckn
        t = lax.broadcasted_iota(I32, lg.shape, 0) // H_B
        c = lax.broadcasted_iota(I32, lg.shape, 1)
        _softmax_update(lg, (c <= t) & (c < n_new), vnew_s[...].astype(BF16), m_s, l_s, acc_s)
        o_ref[0] = acc_s[...] / l_s[...]


def fox_attn_decode(cache_k, cache_v, layer, page_table, qbd, k_new, v_new, cq_rows, ck, ck_new, n_group=8):
    n_b, n_pages = page_table.shape
    rows = qbd.shape[1]
    kw = H_B * HEAD_DIM
    steps = n_pages // n_group
    gw = n_group * PAGE_SIZE

    def page_map(g):
        return lambda b, p, pt: (layer, pt[b, p * n_group + g], 0, 0, 0)

    page_spec = [pl.BlockSpec((None, None, PAGE_SIZE, H_B, HEAD_DIM), page_map(g)) for g in range(n_group)]
    grid_spec = pltpu.PrefetchScalarGridSpec(
        num_scalar_prefetch=1,
        grid=(n_b, steps),
        in_specs=page_spec + page_spec
        + [pl.BlockSpec((1, rows, kw), lambda b, p, pt: (b, 0, 0)),
           pl.BlockSpec((1,) + k_new.shape[1:], lambda b, p, pt: (b, 0, 0)),
           pl.BlockSpec((1,) + v_new.shape[1:], lambda b, p, pt: (b, 0, 0)),
           pl.BlockSpec((1, rows, 1), lambda b, p, pt: (b, 0, 0)),
           pl.BlockSpec((1, H_B, gw), lambda b, p, pt: (b, 0, p)),
           pl.BlockSpec((1, H_B, PAGE_SIZE), lambda b, p, pt: (b, 0, 0))],
        out_specs=pl.BlockSpec((1, rows, kw), lambda b, p, pt: (b, 0, 0)),
        scratch_shapes=[pltpu.VMEM((rows, 1), F32), pltpu.VMEM((rows, 1), F32), pltpu.VMEM((rows, kw), F32),
                        pltpu.VMEM((PAGE_SIZE, kw), F32), pltpu.VMEM((PAGE_SIZE, kw), F32)],
    )
    return pl.pallas_call(
        functools.partial(_fox_attn_decode_kernel, n_group=n_group),
        grid_spec=grid_spec,
        out_shape=jax.ShapeDtypeStruct((n_b, rows, kw), F32),
        compiler_params=_cparams(("parallel", "arbitrary")),
        name="fox_attn_decode",
    )(page_table, *([cache_k] * n_group), *([cache_v] * n_group), qbd, k_new, v_new, cq_rows, ck, ck_new)


def _sb_attn_decode_kernel(pt_ref, *refs, n_group):
    k_refs = refs[:n_group]
    v_refs = refs[n_group:2 * n_group]
    qbd_ref, knew_ref, vnew_ref, o_ref, carry_s, acc_s, knew_s, vnew_s = refs[2 * n_group:]
    p = pl.program_id(1)
    qbd = qbd_ref[0].astype(BF16)
    tri2 = _strict_lower_twice(PAGE_SIZE)

    def page(k_f32, v_f32, mask):
        z = _nt_dot(qbd, k_f32.astype(BF16)) * ATTN_SCALE
        soft = jnp.log1p(jnp.exp(-jnp.abs(z)))
        log_keep = -(jnp.maximum(z, 0.0) + soft)
        log_hit = -(jnp.maximum(-z, 0.0) + soft)
        if mask is not None:
            log_keep = jnp.where(mask, log_keep, 0.0)
        after = _split_suffix(log_keep, tri2) + carry_s[...]
        w = jnp.exp(log_hit + after)
        if mask is not None:
            w = jnp.where(mask, w, 0.0)
        acc_s[...] += jnp.dot(w.astype(BF16), v_f32.astype(BF16), preferred_element_type=F32)
        carry_s[...] += jnp.sum(log_keep, axis=-1, keepdims=True)

    @pl.when(p == 0)
    def _():
        carry_s[...] = jnp.zeros_like(carry_s)
        acc_s[...] = jnp.zeros_like(acc_s)
        n_new = knew_ref.shape[1]
        _fill_new_page(knew_s, knew_ref[0])
        _fill_new_page(vnew_s, vnew_ref[0])
        t = lax.broadcasted_iota(I32, (qbd.shape[0], PAGE_SIZE), 0) // H_C
        c = lax.broadcasted_iota(I32, (qbd.shape[0], PAGE_SIZE), 1)
        page(knew_s[...], vnew_s[...], (c < t) & (c < n_new))

    for g in range(n_group):
        page(_page_slab(k_refs[g]), _page_slab(v_refs[g]), None)

    @pl.when(p == pl.num_programs(1) - 1)
    def _():
        o_ref[0] = acc_s[...]


def sb_attn_decode(cache_k, cache_v, layer, page_table, qbd, k_new, v_new, n_group=4):
    n_b, n_pages = page_table.shape
    rows = qbd.shape[1]
    kw = H_C * HEAD_DIM
    steps = n_pages // n_group

    def page_map(g):
        return lambda b, p, pt: (layer, pt[b, n_pages - 1 - (p * n_group + g)], 0, 0, 0)

    page_spec = [pl.BlockSpec((None, None, PAGE_SIZE, H_C, HEAD_DIM), page_map(g)) for g in range(n_group)]
    grid_spec = pltpu.PrefetchScalarGridSpec(
        num_scalar_prefetch=1,
        grid=(n_b, steps),
        in_specs=page_spec + page_spec
        + [pl.BlockSpec((1, rows, kw), lambda b, p, pt: (b, 0, 0)),
           pl.BlockSpec((1,) + k_new.shape[1:], lambda b, p, pt: (b, 0, 0)),
           pl.BlockSpec((1,) + v_new.shape[1:], lambda b, p, pt: (b, 0, 0))],
        out_specs=pl.BlockSpec((1, rows, kw), lambda b, p, pt: (b, 0, 0)),
        scratch_shapes=[pltpu.VMEM((rows, 1), F32), pltpu.VMEM((rows, kw), F32),
                        pltpu.VMEM((PAGE_SIZE, kw), F32), pltpu.VMEM((PAGE_SIZE, kw), F32)],
    )
    return pl.pallas_call(
        functools.partial(_sb_attn_decode_kernel, n_group=n_group),
        grid_spec=grid_spec,
        out_shape=jax.ShapeDtypeStruct((n_b, rows, kw), F32),
        compiler_params=_cparams(("parallel", "arbitrary")),
        name="sb_attn_decode",
    )(page_table, *([cache_k] * n_group), *([cache_v] * n_group), qbd, k_new, v_new)


def _block_diag_queries(q, group):
    n_b, n_q, heads, hd = q.shape
    onehot = jax.nn.one_hot(jnp.arange(heads) // group, heads // group, dtype=q.dtype)
    out = q[:, :, :, None, :] * onehot[None, None, :, :, None]
    return out.reshape(n_b, n_q * heads, (heads // group) * hd)


def _take_diag(o, n_q, heads, group):
    n_b = o.shape[0]
    groups = heads // group
    onehot = jax.nn.one_hot(jnp.arange(heads) // group, groups, dtype=o.dtype)
    o = o.reshape(n_b, n_q, heads, groups, HEAD_DIM)
    return jnp.sum(o * onehot[None, None, :, :, None], axis=3).reshape(n_b * n_q, heads * HEAD_DIM)


def _ab_weight(w):
    sizes = (H_A * HEAD_DIM, KV_A * HEAD_DIM, KV_A * HEAD_DIM, IDX_HEADS * IDX_DIM, IDX_DIM, IDX_HEADS,
             H_B * HEAD_DIM, H_B * HEAD_DIM, H_B * HEAD_DIM, H_B)
    offs = [0]
    for sz in sizes:
        offs.append(offs[-1] + sz)
    part = [w[:, offs[i]:offs[i + 1]] for i in range(len(sizes))]
    qa, ka, va, iq, ik, iw, qb, kb, vb, fl = part
    pad = jnp.zeros((w.shape[0], LANES - IDX_DIM - IDX_HEADS - H_B), w.dtype)
    return jnp.concatenate([qa, ka, va, iq, ik, iw, fl, pad, qb, kb, vb], axis=1).astype(BF16)


def kernel(x_prompt, x_sample, cache_a_k, cache_a_v, cache_a_idxk, cache_b_k, cache_b_v, cache_b_logf, cache_c_k, cache_c_v, cache_mem_k, cache_mem_v, page_table, mem_prompt, w_in_ab, b_forget, w_out_ab, w_in_c, w_out_c, rel_bias, w_xq, w_xkv, w_xo, ln_g, ln_b, w_router, b_router, w_gate_up, b_gate_up, w_down, b_down):
    n_b, t, d = x_prompt.shape
    s_b, s_t, _ = x_sample.shape
    n_p = n_b * t
    n_s = s_b * s_t
    n_mem = mem_prompt.shape[1]
    past = page_table.shape[1] * PAGE_SIZE
    kvw = KV_A * HEAD_DIM
    xw = XA_HEADS * XA_HEAD_DIM

    x = jnp.concatenate([x_prompt.reshape(n_p, d), x_sample.reshape(n_s, d)], axis=0)
    win = bias_window(rel_bias)
    far = rel_bias[N_BUCKETS - 1]
    brow = bias_rows(rel_bias, past, s_t, past + PAGE_SIZE)
    mem_p = mem_prompt.reshape(n_b * n_mem, d)

    rows_ab_p, rows_ab_s, rows_c_p, rows_c_s, mem_kv = [], [], [], [], []
    for li in range(DEPTH):
        j = li // 2
        if li % 2 == 0:
            bf_row = jnp.zeros((1, LANES), F32).at[0, MISC_FL:MISC_FL + H_B].set(b_forget[j])
            qa, kva, iq, misc, qb, kb, vb = ab_proj(x, _ab_weight(w_in_ab[j]), bf_row)
            o_a_p = dsa_prompt(qa, iq, misc, kva, win, far, n_b, t)
            cum, cumt = cumsum_prompt(misc, n_b, t)
            o_b_p = fox_prompt(qb, kb, vb, cum, cumt, n_b, t)
            misc_s = misc[n_p:].reshape(s_b, s_t, LANES)
            ik_new = misc_s[:, :, MISC_IK:MISC_IK + IDX_DIM]
            iw_rows = misc_s[:, :, MISC_IW:MISC_IW + IDX_HEADS].reshape(s_b, s_t * IDX_HEADS, 1)
            logf_new = misc_s[:, :, MISC_FL:MISC_FL + H_B]
            iq_rows = iq[n_p:].reshape(s_b, s_t * IDX_HEADS, IDX_DIM)
            kva_s = kva[n_p:].reshape(s_b, s_t, 2 * kvw)
            sc_past, sc_new = dsa_scores_decode(cache_a_idxk, j, page_table, iq_rows, iw_rows, ik_new)
            scores = jnp.concatenate([sc_past, sc_new], axis=-1).reshape(n_s, past + PAGE_SIZE)
            sel = topk_select(scores, min(TOPK_MAX, (past + s_t) // 4)).reshape(s_b, s_t, past + PAGE_SIZE)
            qbd_a = _block_diag_queries(qa[n_p:].reshape(s_b, s_t, H_A, HEAD_DIM), H_A // KV_A)
            o_a_s = _take_diag(dsa_attn_decode(cache_a_k, cache_a_v, j, page_table, qbd_a, kva_s, sel, brow),
                               s_t, H_A, H_A // KV_A)
            ck, ck_new = cumsum_pages(cache_b_logf, j, page_table, logf_new)
            cq_rows = jnp.swapaxes(ck_new[:, :, :s_t], 1, 2).reshape(s_b, s_t * H_B, 1)
            qbd_b = _block_diag_queries(qb[n_p:].reshape(s_b, s_t, H_B, HEAD_DIM), 1)
            kb_s = kb[n_p:].reshape(s_b, s_t, H_B * HEAD_DIM)
            vb_s = vb[n_p:].reshape(s_b, s_t, H_B * HEAD_DIM)
            o_b_s = _take_diag(fox_attn_decode(cache_b_k, cache_b_v, j, page_table, qbd_b, kb_s, vb_s,
                                               cq_rows, ck, ck_new), s_t, H_B, 1)
            o_a = jnp.concatenate([o_a_p, o_a_s], axis=0)
            o_b = jnp.concatenate([o_b_p, o_b_s], axis=0)
            w_out = w_out_ab[j].astype(BF16)
            x = linear2_res_ln(o_a, w_out[:H_A * HEAD_DIM], o_b, w_out[H_A * HEAD_DIM:], x,
                               ln_g[li, 0], ln_b[li, 0])
            rows = (kva[:, :kvw], kva[:, kvw:], misc[:, MISC_IK:MISC_IK + IDX_DIM], kb, vb,
                    misc[:, MISC_FL:MISC_FL + H_B])
            tails = ((KV_A, HEAD_DIM), (KV_A, HEAD_DIM), (IDX_DIM,), (H_B, HEAD_DIM), (H_B, HEAD_DIM), (H_B,))
            rows_ab_p.append(tuple(r[:n_p].reshape(n_b, t, *tl) for r, tl in zip(rows, tails)))
            rows_ab_s.append(tuple(r[n_p:].reshape(s_b, s_t, *tl) for r, tl in zip(rows, tails)))
        else:
            q, k, v = linear(x, w_in_c[j].astype(BF16), (H_C * HEAD_DIM,) * 3, name="c_proj")
            o_p = sb_prompt(q, k, v, n_b, t)
            qbd = _block_diag_queries(q[n_p:].reshape(s_b, s_t, H_C, HEAD_DIM), 1)
            k_s = k[n_p:].reshape(s_b, s_t, H_C * HEAD_DIM)
            v_s = v[n_p:].reshape(s_b, s_t, H_C * HEAD_DIM)
            o_s = _take_diag(sb_attn_decode(cache_c_k, cache_c_v, j, page_table, qbd, k_s, v_s), s_t, H_C, 1)
            o = jnp.concatenate([o_p, o_s], axis=0)
            x = linear_res_ln(o, w_out_c[j].astype(BF16), x, ln_g[li, 0], ln_b[li, 0], name="c_out")
            rows_c_p.append((k[:n_p].reshape(n_b, t, H_C, HEAD_DIM), v[:n_p].reshape(n_b, t, H_C, HEAD_DIM)))
            rows_c_s.append((k_s.reshape(s_b, s_t, H_C, HEAD_DIM), v_s.reshape(s_b, s_t, H_C, HEAD_DIM)))

        mk_p, mv_p = linear(mem_p, w_xkv[li].astype(BF16), (xw, xw), tm=256, name="mem_proj")
        mem_kv.append((mk_p, mv_p))
        (xq,) = linear(x, w_xq[li].astype(BF16), (xw,), name="xq_proj")
        o = xattn(xq, mk_p.reshape(n_b, n_mem, xw), mv_p.reshape(n_b, n_mem, xw), 0, t, 1)
        o = xattn(xq, cache_mem_k[li].reshape(s_b, n_mem, xw), cache_mem_v[li].reshape(s_b, n_mem, xw),
                  n_p, s_t, 8, prev=o)
        x = linear_res_ln(o, w_xo[li].astype(BF16), x, ln_g[li, 1], ln_b[li, 1], name="xo_proj")

        y = moe(x, w_router, b_router, w_gate_up, b_gate_up, w_down, b_down, li)
        x = res_ln(y, x, ln_g[li, 2], ln_b[li, 2])

    def stack(rows):
        return tuple(jnp.stack([r[i] for r in rows]) for i in range(len(rows[0])))

    p_mem_k = jnp.stack([kv[0].reshape(n_b, n_mem, XA_HEADS, XA_HEAD_DIM) for kv in mem_kv])
    p_mem_v = jnp.stack([kv[1].reshape(n_b, n_mem, XA_HEADS, XA_HEAD_DIM) for kv in mem_kv])
    return ((x[:n_p].reshape(n_b, t, d), x[n_p:].reshape(s_b, s_t, d))
            + stack(rows_ab_p) + stack(rows_c_p) + (p_mem_k, p_mem_v) + stack(rows_ab_s) + stack(rows_c_s))
```

```python
import functools
import math

import jax
import jax.numpy as jnp
from jax import lax
from jax.experimental import pallas as pl
from jax.experimental.pallas import tpu as pltpu

F32 = jnp.float32
BF16 = jnp.bfloat16
I32 = jnp.int32

D_MODEL = 1024
DEPTH = 4
PAGE_SIZE = 128
HEAD_DIM = 64
H_A = 8
KV_A = 2
IDX_HEADS = 8
IDX_DIM = 64
TOPK_MAX = 256
H_B = 8
H_C = 16
N_BUCKETS = 32
MAX_DISTANCE = 128
XA_HEADS = 4
XA_HEAD_DIM = 128
N_EXPERTS = 32
TOP_K = 4
D_EXPERT = D_MODEL
SWIGLU_LIMIT = 7.0
SWIGLU_ALPHA = 1.702
LN_EPS = 1e-5
NEG_INF = -1e30
ATTN_SCALE = HEAD_DIM ** -0.5
DN_ALPHA = (2 * DEPTH) ** 0.25

LANES = 128
Q_TILE = 128
ROW_TILE = 384
MOE_TILE = 256
VMEM_LIMIT = 56 * 1024 * 1024

MISC_IK = 0
MISC_IW = IDX_DIM
MISC_FL = IDX_DIM + IDX_HEADS

_HI = lax.Precision.HIGHEST


def _cparams(sem):
    return pltpu.CompilerParams(dimension_semantics=sem, vmem_limit_bytes=VMEM_LIMIT)


def _log_sigmoid(x):
    return -(jnp.maximum(-x, 0.0) + jnp.log1p(jnp.exp(-jnp.abs(x))))


def _nt_dot(a, b):
    return lax.dot_general(a, b, (((1,), (1,)), ((), ())), preferred_element_type=F32)


def _linear_kernel(x_ref, w_ref, *o_refs):
    xb = x_ref[...].astype(BF16)
    off = 0
    for o in o_refs:
        n = o.shape[-1]
        o[...] = jnp.dot(xb, w_ref[:, off:off + n], preferred_element_type=F32)
        off += n


def linear(x, w_bf16, splits, tm=ROW_TILE, name="linear"):
    m, k = x.shape
    n = w_bf16.shape[1]
    assert sum(splits) == n and all(s % LANES == 0 for s in splits) and m % tm == 0
    return pl.pallas_call(
        _linear_kernel,
        grid=(m // tm,),
        in_specs=[pl.BlockSpec((tm, k), lambda i: (i, 0)),
                  pl.BlockSpec((k, n), lambda i: (0, 0))],
        out_specs=[pl.BlockSpec((tm, s), lambda i: (i, 0)) for s in splits],
        out_shape=[jax.ShapeDtypeStruct((m, s), F32) for s in splits],
        compiler_params=_cparams(("parallel",)),
        name=name,
    )(x, w_bf16)


def _ab_proj_kernel(x_ref, w_ref, bf_ref, qa_ref, kva_ref, iq_ref, misc_ref, qb_ref, kb_ref, vb_ref):
    xb = x_ref[...].astype(BF16)
    off = 0
    for o in (qa_ref, kva_ref, iq_ref, misc_ref, qb_ref, kb_ref, vb_ref):
        n = o.shape[-1]
        r = jnp.dot(xb, w_ref[:, off:off + n], preferred_element_type=F32)
        if o is misc_ref:
            lane = lax.broadcasted_iota(I32, r.shape, 1)
            is_fl = (lane >= MISC_FL) & (lane < MISC_FL + H_B)
            r = jnp.where(is_fl, _log_sigmoid(r + bf_ref[...]), r)
        o[...] = r
        off += n


def ab_proj(x, w_bf16, bf_row, tm=ROW_TILE):
    m, k = x.shape
    splits = (H_A * HEAD_DIM, 2 * KV_A * HEAD_DIM, IDX_HEADS * IDX_DIM, LANES,
              H_B * HEAD_DIM, H_B * HEAD_DIM, H_B * HEAD_DIM)
    n = sum(splits)
    return pl.pallas_call(
        _ab_proj_kernel,
        grid=(m // tm,),
        in_specs=[pl.BlockSpec((tm, k), lambda i: (i, 0)),
                  pl.BlockSpec((k, n), lambda i: (0, 0)),
                  pl.BlockSpec((1, LANES), lambda i: (0, 0))],
        out_specs=[pl.BlockSpec((tm, s), lambda i: (i, 0)) for s in splits],
        out_shape=[jax.ShapeDtypeStruct((m, s), F32) for s in splits],
        compiler_params=_cparams(("parallel",)),
        name="ab_proj",
    )(x, w_bf16, bf_row)


def _layer_norm(z, g, b):
    mu = jnp.mean(z, axis=-1, keepdims=True)
    zc = z - mu
    var = jnp.mean(zc * zc, axis=-1, keepdims=True)
    return zc * lax.rsqrt(var + LN_EPS) * g + b


def _linear_res_ln_kernel(h_ref, w_ref, x_ref, g_ref, b_ref, o_ref):
    y = jnp.dot(h_ref[...].astype(BF16), w_ref[...], preferred_element_type=F32)
    o_ref[...] = _layer_norm(DN_ALPHA * x_ref[...] + y, g_ref[...], b_ref[...])


def linear_res_ln(h, w_bf16, x, g, b, tm=ROW_TILE, name="linear_res_ln"):
    m, k = h.shape
    d = w_bf16.shape[1]
    return pl.pallas_call(
        _linear_res_ln_kernel,
        grid=(m // tm,),
        in_specs=[pl.BlockSpec((tm, k), lambda i: (i, 0)),
                  pl.BlockSpec((k, d), lambda i: (0, 0)),
                  pl.BlockSpec((tm, d), lambda i: (i, 0)),
                  pl.BlockSpec((1, d), lambda i: (0, 0)),
                  pl.BlockSpec((1, d), lambda i: (0, 0))],
        out_specs=pl.BlockSpec((tm, d), lambda i: (i, 0)),
        out_shape=jax.ShapeDtypeStruct((m, d), F32),
        compiler_params=_cparams(("parallel",)),
        name=name,
    )(h, w_bf16, x, g.reshape(1, d), b.reshape(1, d))


def _linear2_res_ln_kernel(h1_ref, w1_ref, h2_ref, w2_ref, x_ref, g_ref, b_ref, o_ref):
    y = jnp.dot(h1_ref[...].astype(BF16), w1_ref[...], preferred_element_type=F32)
    y = y + jnp.dot(h2_ref[...].astype(BF16), w2_ref[...], preferred_element_type=F32)
    o_ref[...] = _layer_norm(DN_ALPHA * x_ref[...] + y, g_ref[...], b_ref[...])


def linear2_res_ln(h1, w1, h2, w2, x, g, b, tm=ROW_TILE):
    m, k1 = h1.shape
    k2 = h2.shape[1]
    d = w1.shape[1]
    return pl.pallas_call(
        _linear2_res_ln_kernel,
        grid=(m // tm,),
        in_specs=[pl.BlockSpec((tm, k1), lambda i: (i, 0)),
                  pl.BlockSpec((k1, d), lambda i: (0, 0)),
                  pl.BlockSpec((tm, k2), lambda i: (i, 0)),
                  pl.BlockSpec((k2, d), lambda i: (0, 0)),
                  pl.BlockSpec((tm, d), lambda i: (i, 0)),
                  pl.BlockSpec((1, d), lambda i: (0, 0)),
                  pl.BlockSpec((1, d), lambda i: (0, 0))],
        out_specs=pl.BlockSpec((tm, d), lambda i: (i, 0)),
        out_shape=jax.ShapeDtypeStruct((m, d), F32),
        compiler_params=_cparams(("parallel",)),
        name="ab_out",
    )(h1, w1, h2, w2, x, g.reshape(1, d), b.reshape(1, d))


def _t5_bucket(rel):
    n = jnp.maximum(rel, 0)
    max_exact = N_BUCKETS // 2
    nf = jnp.maximum(n, 1).astype(F32)
    large = max_exact + (jnp.log(nf / max_exact) / math.log(MAX_DISTANCE / max_exact)
                         * (N_BUCKETS - max_exact)).astype(I32)
    large = jnp.minimum(large, N_BUCKETS - 1)
    return jnp.where(n < max_exact, n, large)


def _bias_lookup(bucket, tab_col):
    out = jnp.zeros(bucket.shape, F32)
    for b in range(N_BUCKETS):
        out = jnp.where(bucket == b, tab_col[:, b:b + 1], out)
    return out


def _bias_window_kernel(tab_ref, o_ref):
    r = lax.broadcasted_iota(I32, (Q_TILE, 2 * Q_TILE), 0)
    c = lax.broadcasted_iota(I32, (Q_TILE, 2 * Q_TILE), 1)
    bucket = _t5_bucket(r + Q_TILE - c)
    tab_col = jnp.broadcast_to(tab_ref[0], (Q_TILE, N_BUCKETS))
    o_ref[0] = _bias_lookup(bucket, tab_col)


def bias_window(rel_bias):
    tab_t = rel_bias.T.reshape(H_A, 1, N_BUCKETS)
    return pl.pallas_call(
        _bias_window_kernel,
        grid=(H_A,),
        in_specs=[pl.BlockSpec((1, 1, N_BUCKETS), lambda h: (h, 0, 0))],
        out_specs=pl.BlockSpec((1, Q_TILE, 2 * Q_TILE), lambda h: (h, 0, 0)),
        out_shape=jax.ShapeDtypeStruct((H_A, Q_TILE, 2 * Q_TILE), F32),
        compiler_params=_cparams(("parallel",)),
        name="bias_window",
    )(tab_t)


def _bias_rows_kernel(tab_ref, o_ref, *, past, n_q):
    rows, cols = o_ref.shape
    t = lax.broadcasted_iota(I32, (rows, cols), 0) // H_A
    s = lax.broadcasted_iota(I32, (rows, cols), 1)
    o_ref[...] = _bias_lookup(_t5_bucket(past + t - s), tab_ref[...])


def bias_rows(rel_bias, past, n_q, n_cols):
    tab_rows = jnp.tile(rel_bias.T, (n_q, 1))
    return pl.pallas_call(
        functools.partial(_bias_rows_kernel, past=past, n_q=n_q),
        out_shape=jax.ShapeDtypeStruct((n_q * H_A, n_cols), F32),
        compiler_params=pltpu.CompilerParams(vmem_limit_bytes=VMEM_LIMIT),
        name="bias_rows",
    )(tab_rows)


def _sort_key(score):
    bits = pltpu.bitcast(score + 0.0, I32)
    return bits ^ ((bits >> 31) & 0x7FFFFFFF)


def _lane_total(acc):
    return jnp.dot(acc.astype(BF16), jnp.ones((LANES, LANES), BF16), preferred_element_type=F32)


def _count_tiles(pred, n_tiles, rows):
    acc = jnp.zeros((rows, LANES), F32)
    for j in range(n_tiles):
        acc = acc + jnp.where(pred(j), 1.0, 0.0)
    return _lane_total(acc)


def _topk_threshold(key, k, cut_ref):
    r, n = key.shape
    n_tiles = n // LANES
    assert n % LANES == 0 and n_tiles <= 256 and r % 16 == 0
    kf = float(k)
    hr = r // 2
    halves = [[key[h * hr:(h + 1) * hr, j * LANES:(j + 1) * LANES] for j in range(n_tiles)] for h in range(2)]

    def body(it, ans):
        bit = jnp.left_shift(jnp.int32(1), 31 - it)
        out = []
        for h in range(2):
            cand = ans[h] + bit
            cnt = _count_tiles(lambda j: halves[h][j] >= cand, n_tiles, hr)
            out.append(jnp.where(cnt >= kf, cand, ans[h]))
        return tuple(out)

    init = jnp.full((hr, LANES), -2 ** 31, I32)
    thr_rep = lax.fori_loop(0, 32, body, (init, init))
    need, n_eq = [], []
    for h in range(2):
        need.append(kf - _count_tiles(lambda j: halves[h][j] > thr_rep[h], n_tiles, hr))
        n_eq.append(_count_tiles(lambda j: halves[h][j] == thr_rep[h], n_tiles, hr))
    cut_ref[...] = jnp.full((r, 1), n, I32)
    excess = jnp.maximum(jnp.max(n_eq[0] - need[0]), jnp.max(n_eq[1] - need[1]))

    @pl.when(excess > 0.0)
    def _():
        lane = lax.broadcasted_iota(I32, (hr, LANES), 1)
        n_bits = max(1, (n - 1).bit_length())

        def cbody(it, c):
            bit = jnp.left_shift(jnp.int32(1), n_bits - 1 - it)
            out = []
            for h in range(2):
                cand = c[h] + bit
                cnt = _count_tiles(
                    lambda j: (halves[h][j] == thr_rep[h]) & (lane + j * LANES < cand), n_tiles, hr)
                out.append(jnp.where(cnt < need[h], cand, c[h]))
            return tuple(out)

        zero = jnp.zeros((hr, LANES), I32)
        cut = lax.fori_loop(0, n_bits, cbody, (zero, zero))
        cut_ref[0:hr, :] = cut[0][:, 0:1]
        cut_ref[hr:r, :] = cut[1][:, 0:1]

    return jnp.concatenate([thr_rep[0][:, 0:1], thr_rep[1][:, 0:1]], axis=0)


def _dsa_prompt_kernel(far_ref, qa_ref, iq_ref, mq_ref, kva_ref, mk_ref, win_ref, o_ref,
                       s_ref, cut_ref, *, n_sel):
    i = pl.program_id(1)
    tq, t_keys = s_ref.shape
    qpos = i * tq + lax.broadcasted_iota(I32, (tq, t_keys), 0)
    kpos = lax.broadcasted_iota(I32, (tq, t_keys), 1)
    causal = kpos <= qpos

    ikb = mk_ref[:, MISC_IK:MISC_IK + IDX_DIM].astype(BF16)
    score = jnp.zeros((tq, t_keys), F32)
    for n in range(IDX_HEADS):
        iqn = iq_ref[:, n * IDX_DIM:(n + 1) * IDX_DIM].astype(BF16)
        dots = _nt_dot(iqn, ikb)
        score = score + jnp.maximum(dots, 0.0) * mq_ref[:, MISC_IW + n:MISC_IW + n + 1]
    score = jnp.where(causal, score, -jnp.inf)
    key = _sort_key(score)
    thr = _topk_threshold(key, n_sel, cut_ref)
    sel = ((key > thr) | ((key == thr) & (kpos <= cut_ref[...]))) & causal

    off = pl.multiple_of(jnp.maximum(i - 1, 0) * tq, tq)
    rep = H_A // KV_A
    for h in range(H_A):
        g = h // rep
        qh = (qa_ref[:, h * HEAD_DIM:(h + 1) * HEAD_DIM] * ATTN_SCALE).astype(BF16)
        kg = kva_ref[:, g * HEAD_DIM:(g + 1) * HEAD_DIM].astype(BF16)
        vg = kva_ref[:, (KV_A + g) * HEAD_DIM:(KV_A + g + 1) * HEAD_DIM].astype(BF16)
        s_ref[...] = _nt_dot(qh, kg)
        w = win_ref[h] - far_ref[h]
        w = jnp.where(i == 0, jnp.concatenate([w[:, tq:], w[:, :tq]], axis=1), w)
        s_ref[:, pl.ds(off, 2 * tq)] += w
        logits = jnp.where(sel, s_ref[...], NEG_INF)
        m = jnp.max(logits, axis=-1, keepdims=True)
        p = jnp.exp(logits - m)
        o = jnp.dot(p.astype(BF16), vg, preferred_element_type=F32)
        o_ref[:, h * HEAD_DIM:(h + 1) * HEAD_DIM] = o / jnp.sum(p, axis=-1, keepdims=True)


def dsa_prompt(qa, iq, misc, kva, win, far, n_batch, t):
    nq = t // Q_TILE
    n_sel = min(TOPK_MAX, t // 4)
    grid_spec = pltpu.PrefetchScalarGridSpec(
        num_scalar_prefetch=1,
        grid=(n_batch, nq),
        in_specs=[pl.BlockSpec((Q_TILE, H_A * HEAD_DIM), lambda b, i, far: (b * nq + i, 0)),
                  pl.BlockSpec((Q_TILE, IDX_HEADS * IDX_DIM), lambda b, i, far: (b * nq + i, 0)),
                  pl.BlockSpec((Q_TILE, LANES), lambda b, i, far: (b * nq + i, 0)),
                  pl.BlockSpec((t, 2 * KV_A * HEAD_DIM), lambda b, i, far: (b, 0)),
                  pl.BlockSpec((t, LANES), lambda b, i, far: (b, 0)),
                  pl.BlockSpec((H_A, Q_TILE, 2 * Q_TILE), lambda b, i, far: (0, 0, 0))],
        out_specs=pl.BlockSpec((Q_TILE, H_A * HEAD_DIM), lambda b, i, far: (b * nq + i, 0)),
        scratch_shapes=[pltpu.VMEM((Q_TILE, t), F32), pltpu.VMEM((Q_TILE, 1), I32)],
    )
    return pl.pallas_call(
        functools.partial(_dsa_prompt_kernel, n_sel=n_sel),
        grid_spec=grid_spec,
        out_shape=jax.ShapeDtypeStruct((n_batch * t, H_A * HEAD_DIM), F32),
        compiler_params=_cparams(("parallel", "parallel")),
        name="dsa_prompt",
    )(far, qa, iq, misc, kva, misc, win)


def _fox_prompt_kernel(q_ref, k_ref, v_ref, cum_ref, cumt_ref, o_ref):
    i = pl.program_id(2)
    hp = pl.program_id(1)
    tq = q_ref.shape[0]
    t_keys = k_ref.shape[0]
    qpos = i * tq + lax.broadcasted_iota(I32, (tq, t_keys), 0)
    kpos = lax.broadcasted_iota(I32, (tq, t_keys), 1)
    causal = kpos <= qpos
    lane = lax.broadcasted_iota(I32, cum_ref.shape, 1)
    for hh in range(2):
        sl = slice(hh * HEAD_DIM, (hh + 1) * HEAD_DIM)
        head = 2 * hp + hh
        qh = (q_ref[:, sl] * ATTN_SCALE).astype(BF16)
        kh = k_ref[:, sl].astype(BF16)
        vh = v_ref[:, sl].astype(BF16)
        cq = jnp.sum(jnp.where(lane == MISC_FL + head, cum_ref[...], 0.0), axis=-1, keepdims=True)
        ck = cumt_ref[0, pl.ds(head, 1), :]
        logits = _nt_dot(qh, kh) + cq - ck
        logits = jnp.where(causal, logits, NEG_INF)
        m = jnp.max(logits, axis=-1, keepdims=True)
        p = jnp.exp(logits - m)
        o = jnp.dot(p.astype(BF16), vh, preferred_element_type=F32)
        o_ref[:, sl] = o / jnp.sum(p, axis=-1, keepdims=True)


def fox_prompt(qb, kb, vb, cum, cumt, n_batch, t, tq=2 * Q_TILE):
    assert MISC_FL % 8 == 0
    nq = t // tq
    hw = 2 * HEAD_DIM
    return pl.pallas_call(
        _fox_prompt_kernel,
        grid=(n_batch, H_B // 2, nq),
        in_specs=[pl.BlockSpec((tq, hw), lambda b, h, i: (b * nq + i, h)),
                  pl.BlockSpec((t, hw), lambda b, h, i: (b, h)),
                  pl.BlockSpec((t, hw), lambda b, h, i: (b, h)),
                  pl.BlockSpec((tq, LANES), lambda b, h, i: (b * nq + i, 0)),
                  pl.BlockSpec((1, H_B, t), lambda b, h, i: (b, MISC_FL // H_B, 0))],
        out_specs=pl.BlockSpec((tq, hw), lambda b, h, i: (b * nq + i, h)),
        out_shape=jax.ShapeDtypeStruct((n_batch * t, H_B * HEAD_DIM), F32),
        compiler_params=_cparams(("parallel", "parallel", "parallel")),
        name="fox_prompt",
    )(qb, kb, vb, cum, cumt)


def _split_suffix(lk, tri2):
    hi = lk.astype(BF16)
    lo = (lk - hi.astype(F32)).astype(BF16)
    return jnp.dot(jnp.concatenate([hi, lo], axis=1), tri2, preferred_element_type=F32)


def _strict_lower_twice(n):
    j = lax.broadcasted_iota(I32, (2 * n, n), 0)
    s = lax.broadcasted_iota(I32, (2 * n, n), 1)
    j = jnp.where(j >= n, j - n, j)
    return jnp.where(j > s, 1.0, 0.0).astype(BF16)


def _stick_chunk(z, v_bf16, carry, acc, tri2, mask):
    w_keys = z.shape[1]
    n_sub = w_keys // LANES
    soft = jnp.log1p(jnp.exp(-jnp.abs(z)))
    log_keep = -(jnp.maximum(z, 0.0) + soft)
    log_hit = -(jnp.maximum(-z, 0.0) + soft)
    if mask is not None:
        log_keep = jnp.where(mask, log_keep, 0.0)
    subs = [log_keep[:, s * LANES:(s + 1) * LANES] for s in range(n_sub)]
    sums = [jnp.sum(x, axis=-1, keepdims=True) for x in subs]
    afters = [None] * n_sub
    right = carry
    for s in range(n_sub - 1, -1, -1):
        afters[s] = _split_suffix(subs[s], tri2) + right
        right = right + sums[s]
    w = jnp.exp(log_hit + jnp.concatenate(afters, axis=1))
    if mask is not None:
        w = jnp.where(mask, w, 0.0)
    acc = acc + jnp.dot(w.astype(BF16), v_bf16, preferred_element_type=F32)
    return right, acc


def _sb_prompt_kernel(q_ref, k_ref, v_ref, o_ref, *, chunk):
    i = pl.program_id(2)
    tq = q_ref.shape[0]
    tri2 = _strict_lower_twice(LANES)
    qs = [q_ref[:, hh * HEAD_DIM:(hh + 1) * HEAD_DIM].astype(BF16) for hh in range(2)]

    def chunk_step(c, st, masked):
        base = pl.multiple_of(c * chunk, chunk)
        mask = None
        if masked:
            qpos = i * tq + lax.broadcasted_iota(I32, (tq, chunk), 0)
            kpos = base + lax.broadcasted_iota(I32, (tq, chunk), 1)
            mask = kpos < qpos
        out = []
        for hh in range(2):
            cs = slice(hh * HEAD_DIM, (hh + 1) * HEAD_DIM)
            kh = k_ref[pl.ds(base, chunk), cs].astype(BF16)
            vh = v_ref[pl.ds(base, chunk), cs].astype(BF16)
            z = _nt_dot(qs[hh], kh) * ATTN_SCALE
            out.extend(_stick_chunk(z, vh, st[2 * hh], st[2 * hh + 1], tri2, mask))
        return tuple(out)

    c_diag = (i * tq) // chunk
    zero = (jnp.zeros((tq, 1), F32), jnp.zeros((tq, HEAD_DIM), F32))
    st = chunk_step(c_diag, zero + zero, True)
    st = lax.fori_loop(0, c_diag, lambda step, s_: chunk_step(c_diag - 1 - step, s_, False), st)
    o_ref[:, :HEAD_DIM] = st[1]
    o_ref[:, HEAD_DIM:] = st[3]


def sb_prompt(q, k, v, n_batch, t):
    nq = t // Q_TILE
    hw = 2 * HEAD_DIM
    chunk = min(4 * LANES, t)
    return pl.pallas_call(
        functools.partial(_sb_prompt_kernel, chunk=chunk),
        grid=(n_batch, H_C // 2, nq),
        in_specs=[pl.BlockSpec((Q_TILE, hw), lambda b, h, i: (b * nq + i, h)),
                  pl.BlockSpec((t, hw), lambda b, h, i: (b, h)),
                  pl.BlockSpec((t, hw), lambda b, h, i: (b, h))],
        out_specs=pl.BlockSpec((Q_TILE, hw), lambda b, h, i: (b * nq + i, h)),
        out_shape=jax.ShapeDtypeStruct((n_batch * t, H_C * HEAD_DIM), F32),
        compiler_params=_cparams(("parallel", "parallel", "parallel")),
        name="sb_prompt",
    )(q, k, v)


def _cumsum_prompt_kernel(m_ref, cum_ref, cumt_ref, carry_ref):
    i = pl.program_id(1)
    tb = m_ref.shape[0]

    @pl.when(i == 0)
    def _():
        carry_ref[...] = jnp.zeros_like(carry_ref)

    r = lax.broadcasted_iota(I32, (tb, tb), 0)
    c = lax.broadcasted_iota(I32, (tb, tb), 1)
    lower = jnp.where(c <= r, 1.0, 0.0)
    cum = jnp.dot(lower, m_ref[...], precision=_HI, preferred_element_type=F32) + carry_ref[...]
    cum_ref[...] = cum
    carry_ref[...] = cum[tb - 1:tb, :]
    eye = jnp.where(lax.broadcasted_iota(I32, (LANES, LANES), 0) == lax.broadcasted_iota(I32, (LANES, LANES), 1),
                    1.0, 0.0)
    cumt_ref[0] = lax.dot_general(eye, cum, (((1,), (1,)), ((), ())), precision=_HI,
                                  preferred_element_type=F32)


def cumsum_prompt(misc, n_batch, t, tb=256):
    nb = t // tb
    return pl.pallas_call(
        _cumsum_prompt_kernel,
        grid=(n_batch, nb),
        in_specs=[pl.BlockSpec((tb, LANES), lambda b, i: (b * nb + i, 0))],
        out_specs=[pl.BlockSpec((tb, LANES), lambda b, i: (b * nb + i, 0)),
                   pl.BlockSpec((1, LANES, tb), lambda b, i: (b, 0, i))],
        out_shape=[jax.ShapeDtypeStruct((n_batch * t, LANES), F32),
                   jax.ShapeDtypeStruct((n_batch, LANES, t), F32)],
        scratch_shapes=[pltpu.VMEM((1, LANES), F32)],
        compiler_params=_cparams(("parallel", "arbitrary")),
        name="cumsum_prompt",
    )(misc)


def _cumsum_pages_kernel(pt_ref, *refs, n_group):
    page_refs = refs[:n_group]
    new_ref, cum_ref, cumnew_ref, carry_ref = refs[n_group:]
    p = pl.program_id(1)

    @pl.when(p == 0)
    def _():
        carry_ref[...] = jnp.zeros_like(carry_ref)

    r = lax.broadcasted_iota(I32, (PAGE_SIZE, PAGE_SIZE), 0)
    c = lax.broadcasted_iota(I32, (PAGE_SIZE, PAGE_SIZE), 1)
    upper = jnp.where(r <= c, 1.0, 0.0)
    carry = carry_ref[...]
    within = [jnp.dot(r_[...], upper, precision=_HI, preferred_element_type=F32) for r_ in page_refs]
    for g in range(n_group):
        cum = within[g] + carry
        cum_ref[0, :, g * PAGE_SIZE:(g + 1) * PAGE_SIZE] = cum
        carry = cum[:, PAGE_SIZE - 1:PAGE_SIZE]
    carry_ref[...] = carry

    @pl.when(p == pl.num_programs(1) - 1)
    def _():
        n_new = new_ref.shape[1]
        eye = jnp.where(lax.broadcasted_iota(I32, (H_B, H_B), 0) == lax.broadcasted_iota(I32, (H_B, H_B), 1),
                        1.0, 0.0)
        newt = lax.dot_general(eye, new_ref[0], (((1,), (1,)), ((), ())), precision=_HI,
                               preferred_element_type=F32)
        lane = lax.broadcasted_iota(I32, (H_B, LANES), 1)
        out = jnp.zeros((H_B, LANES), F32)
        run = carry
        for s in range(n_new):
            run = run + newt[:, s:s + 1]
            out = jnp.where(lane == s, run, out)
        cumnew_ref[0] = out


def cumsum_pages(logf_t, layer, page_table, logf_new, n_group=32):
    n_b, n_pages = page_table.shape
    steps = n_pages // n_group

    def page_map(g):
        return lambda b, p, pt: (layer, pt[b, p * n_group + g], 0, 0)

    grid_spec = pltpu.PrefetchScalarGridSpec(
        num_scalar_prefetch=1,
        grid=(n_b, steps),
        in_specs=[pl.BlockSpec((None, None, H_B, PAGE_SIZE), page_map(g)) for g in range(n_group)]
        + [pl.BlockSpec((1, logf_new.shape[1], H_B), lambda b, p, pt: (b, 0, 0))],
        out_specs=[pl.BlockSpec((1, H_B, n_group * PAGE_SIZE), lambda b, p, pt: (b, 0, p)),
                   pl.BlockSpec((1, H_B, LANES), lambda b, p, pt: (b, 0, 0))],
        scratch_shapes=[pltpu.VMEM((H_B, 1), F32)],
    )
    return pl.pallas_call(
        functools.partial(_cumsum_pages_kernel, n_group=n_group),
        grid_spec=grid_spec,
        out_shape=[jax.ShapeDtypeStruct((n_b, H_B, n_pages * PAGE_SIZE), F32),
                   jax.ShapeDtypeStruct((n_b, H_B, LANES), F32)],
        compiler_params=_cparams(("parallel", "arbitrary")),
        name="cumsum_pages",
    )(page_table, *([logf_t] * n_group), logf_new)


def _xattn_kernel(q_ref, k_ref, v_ref, *rest, rows):
    o_ref = rest[-1]
    scale = XA_HEAD_DIM ** -0.5
    for bi in range(k_ref.shape[0]):
        rs = slice(bi * rows, (bi + 1) * rows)
        for h in range(XA_HEADS):
            cs = slice(h * XA_HEAD_DIM, (h + 1) * XA_HEAD_DIM)
            q = q_ref[rs, cs].astype(BF16)
            k = k_ref[bi, :, cs].astype(BF16)
            v = v_ref[bi, :, cs].astype(BF16)
            logits = _nt_dot(q, k) * scale
            m = jnp.max(logits, axis=-1, keepdims=True)
            p = jnp.exp(logits - m)
            p = p / jnp.sum(p, axis=-1, keepdims=True)
            o_ref[rs, cs] = jnp.dot(p.astype(BF16), v, preferred_element_type=F32)


def xattn(q, mem_k, mem_v, row0, rows, n_bblk, prev=None):
    n_tok, dq = q.shape
    n_b, n_mem, _ = mem_k.shape
    if rows >= 8:
        tq = min(rows, 512)
        per = rows // tq
        steps = n_b * per
        q_spec = pl.BlockSpec((tq, dq), lambda i: (row0 // tq + i, 0))
        m_spec = pl.BlockSpec((1, n_mem, dq), lambda i: (i // per, 0, 0))
        kern_rows = tq
    else:
        tq = rows * n_bblk
        steps = n_b // n_bblk
        q_spec = pl.BlockSpec((tq, dq), lambda i: (row0 // tq + i, 0))
        m_spec = pl.BlockSpec((n_bblk, n_mem, dq), lambda i: (i, 0, 0))
        kern_rows = rows
    assert row0 % tq == 0
    in_specs = [q_spec, m_spec, m_spec]
    args = [q, mem_k, mem_v]
    aliases = {}
    if prev is not None:
        in_specs.append(pl.BlockSpec(memory_space=pl.ANY))
        args.append(prev)
        aliases = {3: 0}
    return pl.pallas_call(
        functools.partial(_xattn_kernel, rows=kern_rows),
        grid=(steps,),
        in_specs=in_specs,
        out_specs=q_spec,
        out_shape=jax.ShapeDtypeStruct((n_tok, dq), F32),
        input_output_aliases=aliases,
        compiler_params=_cparams(("parallel",)),
        name="xattn",
    )(*args)


def _router_kernel(x_ref, w_ref, b_ref, idx_ref, gate_ref, rank_ref, cnt_ref, carry_s):
    i = pl.program_id(0)

    @pl.when(i == 0)
    def _():
        carry_s[...] = jnp.zeros_like(carry_s)

    logits = jnp.dot(x_ref[...], w_ref[...], precision=_HI, preferred_element_type=F32) + b_ref[...]
    tm = logits.shape[0]
    lane = lax.broadcasted_iota(I32, logits.shape, 1)
    lane_k = lax.broadcasted_iota(I32, (tm, TOP_K), 1)
    vals = jnp.zeros((tm, TOP_K), F32)
    idxs = jnp.zeros((tm, TOP_K), I32)
    picks = []
    cur = logits
    for r in range(TOP_K):
        m = jnp.max(cur, axis=-1, keepdims=True)
        ix = jnp.min(jnp.where(cur == m, lane, N_EXPERTS), axis=-1, keepdims=True)
        vals = jnp.where(lane_k == r, m, vals)
        idxs = jnp.where(lane_k == r, ix, idxs)
        picks.append(lane == ix)
        cur = jnp.where(picks[-1], -jnp.inf, cur)
    e = jnp.exp(vals - jnp.max(vals, axis=-1, keepdims=True))
    gate_ref[...] = e / jnp.sum(e, axis=-1, keepdims=True)
    idx_ref[...] = idxs

    chosen = jnp.zeros(logits.shape, F32)
    for pk in picks:
        chosen = jnp.where(pk, 1.0, chosen)
    r_i = lax.broadcasted_iota(I32, (tm, tm), 0)
    c_i = lax.broadcasted_iota(I32, (tm, tm), 1)
    before = jnp.where(c_i < r_i, 1.0, 0.0).astype(BF16)
    base = jnp.dot(before, chosen.astype(BF16), preferred_element_type=F32) + carry_s[...]
    ranks = jnp.zeros((tm, TOP_K), F32)
    for r, pk in enumerate(picks):
        ranks = jnp.where(lane_k == r, jnp.sum(jnp.where(pk, base, 0.0), axis=-1, keepdims=True), ranks)
    rank_ref[...] = ranks.astype(I32)
    carry_s[...] += jnp.sum(chosen, axis=0, keepdims=True)
    cnt_ref[...] = carry_s[...].astype(I32)


def router(x, w_router, b_router, tm=ROW_TILE):
    m, d = x.shape
    return pl.pallas_call(
        _router_kernel,
        grid=(m // tm,),
        in_specs=[pl.BlockSpec((tm, d), lambda i: (i, 0)),
                  pl.BlockSpec((d, N_EXPERTS), lambda i: (0, 0)),
                  pl.BlockSpec((1, N_EXPERTS), lambda i: (0, 0))],
        out_specs=[pl.BlockSpec((tm, TOP_K), lambda i: (i, 0)),
                   pl.BlockSpec((tm, TOP_K), lambda i: (i, 0)),
                   pl.BlockSpec((tm, TOP_K), lambda i: (i, 0)),
                   pl.BlockSpec((1, N_EXPERTS), lambda i: (0, 0))],
        out_shape=[jax.ShapeDtypeStruct((m, TOP_K), I32), jax.ShapeDtypeStruct((m, TOP_K), F32),
                   jax.ShapeDtypeStruct((m, TOP_K), I32), jax.ShapeDtypeStruct((1, N_EXPERTS), I32)],
        scratch_shapes=[pltpu.VMEM((1, N_EXPERTS), F32)],
        compiler_params=_cparams(("arbitrary",)),
        name="router",
    )(x, w_router, b_router.reshape(1, N_EXPERTS))


def _expert_kernel(bexp_ref, bval_ref, x_ref, wgu_ref, bgu_ref, wd_ref, bd_ref, o_ref, wgu_s, wd_s):
    i = pl.program_id(0)
    prev = bexp_ref[jnp.maximum(i - 1, 0)]

    @pl.when((i == 0) | (bexp_ref[i] != prev))
    def _():
        wgu_s[...] = wgu_ref[0].astype(BF16)
        wd_s[...] = wd_ref[0].astype(BF16)

    @pl.when(bval_ref[i] > 0)
    def _():
        xb = x_ref[...].astype(BF16)
        de = wd_s.shape[0]
        g = jnp.dot(xb, wgu_s[:, :de], preferred_element_type=F32) + bgu_ref[0, :, :de]
        u = jnp.dot(xb, wgu_s[:, de:], preferred_element_type=F32) + bgu_ref[0, :, de:]
        g = jnp.minimum(g, SWIGLU_LIMIT)
        u = jnp.clip(u, -SWIGLU_LIMIT, SWIGLU_LIMIT)
        act = g * jax.nn.sigmoid(SWIGLU_ALPHA * g) * (u + 1.0)
        o_ref[...] = jnp.dot(act.astype(BF16), wd_s[...], preferred_element_type=F32) + bd_ref[0]

    @pl.when(bval_ref[i] == 0)
    def _():
        o_ref[...] = jnp.zeros_like(o_ref)


def expert_blocks(x_rows, block_exp, block_valid, w_gu, b_gu, w_down, b_down, layer):
    n_rows, d = x_rows.shape
    n_blocks = n_rows // MOE_TILE
    de = w_down.shape[2]
    grid_spec = pltpu.PrefetchScalarGridSpec(
        num_scalar_prefetch=2,
        grid=(n_blocks,),
        in_specs=[pl.BlockSpec((MOE_TILE, d), lambda i, be, bv: (i, 0)),
                  pl.BlockSpec((None, 1, d, 2 * de), lambda i, be, bv: (layer, be[i], 0, 0)),
                  pl.BlockSpec((None, 1, 1, 2 * de), lambda i, be, bv: (layer, be[i], 0, 0)),
                  pl.BlockSpec((None, 1, de, d), lambda i, be, bv: (layer, be[i], 0, 0)),
                  pl.BlockSpec((None, 1, 1, d), lambda i, be, bv: (layer, be[i], 0, 0))],
        out_specs=pl.BlockSpec((MOE_TILE, d), lambda i, be, bv: (i, 0)),
        scratch_shapes=[pltpu.VMEM((d, 2 * de), BF16), pltpu.VMEM((de, d), BF16)],
    )
    return pl.pallas_call(
        _expert_kernel,
        grid_spec=grid_spec,
        out_shape=jax.ShapeDtypeStruct((n_rows, d), F32),
        compiler_params=_cparams(("arbitrary",)),
        name="experts",
    )(block_exp, block_valid, x_rows, w_gu, b_gu.reshape(*b_gu.shape[:2], 1, 2 * de),
      w_down, b_down.reshape(*b_down.shape[:2], 1, d))


def _combine_ln_kernel(y_ref, gate_ref, x_ref, g_ref, b_ref, o_ref):
    y = y_ref[0] * gate_ref[:, 0:1]
    for k in range(1, TOP_K):
        y = y + y_ref[k] * gate_ref[:, k:k + 1]
    o_ref[...] = _layer_norm(DN_ALPHA * x_ref[...] + y, g_ref[...], b_ref[...])


def combine_ln(y_slots, gate, x, g, b, tm=ROW_TILE):
    m, d = x.shape
    return pl.pallas_call(
        _combine_ln_kernel,
        grid=(m // tm,),
        in_specs=[pl.BlockSpec((TOP_K, tm, d), lambda i: (0, i, 0)),
                  pl.BlockSpec((tm, TOP_K), lambda i: (i, 0)),
                  pl.BlockSpec((tm, d), lambda i: (i, 0)),
                  pl.BlockSpec((1, d), lambda i: (0, 0)),
                  pl.BlockSpec((1, d), lambda i: (0, 0))],
        out_specs=pl.BlockSpec((tm, d), lambda i: (i, 0)),
        out_shape=jax.ShapeDtypeStruct((m, d), F32),
        compiler_params=_cparams(("parallel",)),
        name="moe_combine_ln",
    )(y_slots, gate, x, g.reshape(1, d), b.reshape(1, d))


def moe_ln(x, w_router, b_router, w_gu, b_gu, w_down, b_down, layer, g, b):
    n_tok, d = x.shape
    top_idx, gate, rank, counts = router(x, w_router[layer], b_router[layer])
    counts = counts.reshape(N_EXPERTS)
    n_assign = n_tok * TOP_K
    padded = (counts + MOE_TILE - 1) // MOE_TILE * MOE_TILE
    pad_end = jnp.cumsum(padded)
    pad_start = pad_end - padded
    pos = pad_start[top_idx] + rank
    n_blocks = -(-n_assign // MOE_TILE) + N_EXPERTS
    n_rows = n_blocks * MOE_TILE
    tok = jnp.broadcast_to(jnp.arange(n_tok, dtype=I32)[:, None], (n_tok, TOP_K))
    row_tok = jnp.full((n_rows,), n_tok, I32).at[pos.reshape(-1)].set(tok.reshape(-1))
    block_start = jnp.arange(n_blocks, dtype=I32) * MOE_TILE
    block_exp = jnp.minimum(jnp.searchsorted(pad_end, block_start, side='right'), N_EXPERTS - 1).astype(I32)
    block_valid = (block_start < pad_end[-1]).astype(I32)
    x_pad = jnp.concatenate([x, jnp.zeros((1, d), x.dtype)], axis=0)
    y_rows = expert_blocks(x_pad[row_tok], block_exp, block_valid, w_gu, b_gu, w_down, b_down, layer)
    y_slots = y_rows[pos.T.reshape(-1)].reshape(TOP_K, n_tok, d)
    return combine_ln(y_slots, gate, x, g, b)


def _new_page_t(dst_ref, new_rows):
    dst_ref[...] = jnp.zeros_like(dst_ref)
    dst_ref[0:new_rows.shape[0], :] = new_rows
    return dst_ref[...].T


def _pages(refs):
    return jnp.concatenate([r[...] for r in refs], axis=1)


def _dsa_scores_kernel(pt_ref, *refs, n_group):
    page_refs = refs[:n_group]
    iq_ref, iw_ref, iknew_ref, sc_ref, scnew_ref, new_s = refs[n_group:]
    p = pl.program_id(1)
    iqm = iq_ref[0].astype(BF16)
    iw = iw_ref[0]
    n_q = iqm.shape[0] // IDX_HEADS

    def score_of(ik_t):
        dots = jnp.dot(iqm, ik_t.astype(BF16), preferred_element_type=F32)
        w = jnp.maximum(dots, 0.0) * iw
        return jnp.sum(w.reshape(n_q, IDX_HEADS, ik_t.shape[1]), axis=1)

    sc_ref[0] = score_of(_pages(page_refs))

    @pl.when(p == 0)
    def _():
        n_new = iknew_ref.shape[1]
        s = score_of(_new_page_t(new_s, iknew_ref[0]))
        t = lax.broadcasted_iota(I32, s.shape, 0)
        c = lax.broadcasted_iota(I32, s.shape, 1)
        scnew_ref[0] = jnp.where((c <= t) & (c < n_new), s, -jnp.inf)


def dsa_scores_decode(idxk_t, layer, page_table, iq_rows, iw_rows, ik_new, n_group=32):
    n_b, n_pages = page_table.shape
    n_q = iq_rows.shape[1] // IDX_HEADS
    steps = n_pages // n_group

    def page_map(g):
        return lambda b, p, pt: (layer, pt[b, p * n_group + g], 0, 0)

    grid_spec = pltpu.PrefetchScalarGridSpec(
        num_scalar_prefetch=1,
        grid=(n_b, steps),
        in_specs=[pl.BlockSpec((None, None, IDX_DIM, PAGE_SIZE), page_map(g)) for g in range(n_group)]
        + [pl.BlockSpec((1,) + iq_rows.shape[1:], lambda b, p, pt: (b, 0, 0)),
           pl.BlockSpec((1,) + iw_rows.shape[1:], lambda b, p, pt: (b, 0, 0)),
           pl.BlockSpec((1,) + ik_new.shape[1:], lambda b, p, pt: (b, 0, 0))],
        out_specs=[pl.BlockSpec((1, n_q, n_group * PAGE_SIZE), lambda b, p, pt: (b, 0, p)),
                   pl.BlockSpec((1, n_q, PAGE_SIZE), lambda b, p, pt: (b, 0, 0))],
        scratch_shapes=[pltpu.VMEM((PAGE_SIZE, IDX_DIM), F32)],
    )
    return pl.pallas_call(
        functools.partial(_dsa_scores_kernel, n_group=n_group),
        grid_spec=grid_spec,
        out_shape=[jax.ShapeDtypeStruct((n_b, n_q, n_pages * PAGE_SIZE), F32),
                   jax.ShapeDtypeStruct((n_b, n_q, PAGE_SIZE), F32)],
        compiler_params=_cparams(("parallel", "arbitrary")),
        name="dsa_scores_decode",
    )(page_table, *([idxk_t] * n_group), iq_rows, iw_rows, ik_new)


def _select_kernel(s_ref, sel_ref, cut_ref, *, n_sel):
    key = _sort_key(s_ref[...])
    thr = _topk_threshold(key, n_sel, cut_ref)
    idx = lax.broadcasted_iota(I32, key.shape, 1)
    sel = (key > thr) | ((key == thr) & (idx <= cut_ref[...]))
    sel_ref[...] = jnp.where(sel, 1.0, 0.0)


def topk_select(scores, n_sel):
    return pl.pallas_call(
        functools.partial(_select_kernel, n_sel=n_sel),
        out_shape=jax.ShapeDtypeStruct(scores.shape, F32),
        scratch_shapes=[pltpu.VMEM((scores.shape[0], 1), I32)],
        compiler_params=pltpu.CompilerParams(vmem_limit_bytes=VMEM_LIMIT),
        name="topk_select",
    )(scores)


def _softmax_update(logits, mask, vt_bf16, m_s, l_s, acc_s):
    if mask is not None:
        logits = jnp.where(mask, logits, NEG_INF)
    m_old = m_s[...]
    m_new = jnp.maximum(m_old, jnp.max(logits, axis=-1, keepdims=True))
    alpha = jnp.exp(m_old - m_new)
    pe = jnp.exp(logits - m_new)
    if mask is not None:
        pe = jnp.where(mask, pe, 0.0)
    l_s[...] = alpha * l_s[...] + jnp.sum(pe, axis=-1, keepdims=True)
    acc_s[...] = alpha * acc_s[...] + _nt_dot(pe.astype(BF16), vt_bf16)
    m_s[...] = m_new


def _expand_rows(x, reps):
    n_q, s = x.shape
    return jnp.broadcast_to(x[:, None, :], (n_q, reps, s)).reshape(n_q * reps, s)


def _softmax_init(m_s, l_s, acc_s):
    m_s[...] = jnp.full_like(m_s, NEG_INF)
    l_s[...] = jnp.zeros_like(l_s)
    acc_s[...] = jnp.zeros_like(acc_s)


def _dsa_attn_decode_kernel(pt_ref, *refs, n_group):
    k_refs = refs[:n_group]
    v_refs = refs[n_group:2 * n_group]
    (qbd_ref, kvnew_ref, sel_ref, selnew_ref, bias_ref, biasnew_ref, o_ref,
     m_s, l_s, acc_s, knew_s, vnew_s) = refs[2 * n_group:]
    p = pl.program_id(1)
    kw = KV_A * HEAD_DIM

    @pl.when(p == 0)
    def _():
        _softmax_init(m_s, l_s, acc_s)

    qbd = qbd_ref[0].astype(BF16)
    logits = jnp.dot(qbd, _pages(k_refs).astype(BF16), preferred_element_type=F32) * ATTN_SCALE + bias_ref[...]
    mask = _expand_rows(sel_ref[0], H_A) > 0.5
    _softmax_update(logits, mask, _pages(v_refs).astype(BF16), m_s, l_s, acc_s)

    @pl.when(p == pl.num_programs(1) - 1)
    def _():
        n_new = kvnew_ref.shape[1]
        kt = _new_page_t(knew_s, kvnew_ref[0, :, :kw]).astype(BF16)
        vt = _new_page_t(vnew_s, kvnew_ref[0, :, kw:]).astype(BF16)
        lg = jnp.dot(qbd, kt, preferred_element_type=F32) * ATTN_SCALE + biasnew_ref[...]
        t = lax.broadcasted_iota(I32, lg.shape, 0) // H_A
        c = lax.broadcasted_iota(I32, lg.shape, 1)
        mk = (_expand_rows(selnew_ref[0], H_A) > 0.5) & (c <= t) & (c < n_new)
        _softmax_update(lg, mk, vt, m_s, l_s, acc_s)
        o_ref[0] = acc_s[...] / l_s[...]


def _decode_specs(n_group, width, page_map, rows, extra):
    page_spec = [pl.BlockSpec((None, None, width, PAGE_SIZE), page_map(g)) for g in range(n_group)]
    return page_spec + page_spec + [pl.BlockSpec((1, rows, width), lambda b, p, pt: (b, 0, 0))] + extra


def dsa_attn_decode(k_t, v_t, layer, page_table, qbd, kv_new, sel, bias, n_group=16):
    n_b, n_pages = page_table.shape
    rows = qbd.shape[1]
    n_q = rows // H_A
    kw = KV_A * HEAD_DIM
    steps = n_pages // n_group
    gw = n_group * PAGE_SIZE

    def page_map(g):
        return lambda b, p, pt: (layer, pt[b, p * n_group + g], 0, 0)

    grid_spec = pltpu.PrefetchScalarGridSpec(
        num_scalar_prefetch=1,
        grid=(n_b, steps),
        in_specs=_decode_specs(n_group, kw, page_map, rows, [
            pl.BlockSpec((1,) + kv_new.shape[1:], lambda b, p, pt: (b, 0, 0)),
            pl.BlockSpec((1, n_q, gw), lambda b, p, pt: (b, 0, p)),
            pl.BlockSpec((1, n_q, PAGE_SIZE), lambda b, p, pt: (b, 0, n_pages)),
            pl.BlockSpec((rows, gw), lambda b, p, pt: (0, p)),
            pl.BlockSpec((rows, PAGE_SIZE), lambda b, p, pt: (0, n_pages))]),
        out_specs=pl.BlockSpec((1, rows, kw), lambda b, p, pt: (b, 0, 0)),
        scratch_shapes=[pltpu.VMEM((rows, 1), F32), pltpu.VMEM((rows, 1), F32), pltpu.VMEM((rows, kw), F32),
                        pltpu.VMEM((PAGE_SIZE, kw), F32), pltpu.VMEM((PAGE_SIZE, kw), F32)],
    )
    return pl.pallas_call(
        functools.partial(_dsa_attn_decode_kernel, n_group=n_group),
        grid_spec=grid_spec,
        out_shape=jax.ShapeDtypeStruct((n_b, rows, kw), F32),
        compiler_params=_cparams(("parallel", "arbitrary")),
        name="dsa_attn_decode",
    )(page_table, *([k_t] * n_group), *([v_t] * n_group), qbd, kv_new, sel, sel, bias, bias)


def _fox_attn_decode_kernel(pt_ref, *refs, n_group):
    k_refs = refs[:n_group]
    v_refs = refs[n_group:2 * n_group]
    (qbd_ref, knew_ref, vnew_ref, cq_ref, ck_ref, cknew_ref, o_ref,
     m_s, l_s, acc_s, knew_s, vnew_s) = refs[2 * n_group:]
    p = pl.program_id(1)
    n_q = qbd_ref.shape[1] // H_B

    @pl.when(p == 0)
    def _():
        _softmax_init(m_s, l_s, acc_s)

    qbd = qbd_ref[0].astype(BF16)
    cq = cq_ref[0]
    ck = jnp.concatenate([ck_ref[0]] * n_q, axis=0)
    logits = jnp.dot(qbd, _pages(k_refs).astype(BF16), preferred_element_type=F32) * ATTN_SCALE + cq - ck
    _softmax_update(logits, None, _pages(v_refs).astype(BF16), m_s, l_s, acc_s)

    @pl.when(p == pl.num_programs(1) - 1)
    def _():
        n_new = knew_ref.shape[1]
        kt = _new_page_t(knew_s, knew_ref[0]).astype(BF16)
        vt = _new_page_t(vnew_s, vnew_ref[0]).astype(BF16)
        ckn = jnp.concatenate([cknew_ref[0]] * n_q, axis=0)
        lg = jnp.dot(qbd, kt, preferred_element_type=F32) * ATTN_SCALE + cq - ckn
        t = lax.broadcasted_iota(I32, lg.shape, 0) // H_B
        c = lax.broadcasted_iota(I32, lg.shape, 1)
        _softmax_update(lg, (c <= t) & (c < n_new), vt, m_s, l_s, acc_s)
        o_ref[0] = acc_s[...] / l_s[...]


def fox_attn_decode(k_t, v_t, layer, page_table, qbd, k_new, v_new, cq_rows, ck, ck_new, n_group=8):
    n_b, n_pages = page_table.shape
    rows = qbd.shape[1]
    kw = H_B * HEAD_DIM
    steps = n_pages // n_group
    gw = n_group * PAGE_SIZE

    def page_map(g):
        return lambda b, p, pt: (layer, pt[b, p * n_group + g], 0, 0)

    grid_spec = pltpu.PrefetchScalarGridSpec(
        num_scalar_prefetch=1,
        grid=(n_b, steps),
        in_specs=_decode_specs(n_group, kw, page_map, rows, [
            pl.BlockSpec((1,) + k_new.shape[1:], lambda b, p, pt: (b, 0, 0)),
            pl.BlockSpec((1,) + v_new.shape[1:], lambda b, p, pt: (b, 0, 0)),
            pl.BlockSpec((1, rows, 1), lambda b, p, pt: (b, 0, 0)),
            pl.BlockSpec((1, H_B, gw), lambda b, p, pt: (b, 0, p)),
            pl.BlockSpec((1, H_B, PAGE_SIZE), lambda b, p, pt: (b, 0, 0))]),
        out_specs=pl.BlockSpec((1, rows, kw), lambda b, p, pt: (b, 0, 0)),
        scratch_shapes=[pltpu.VMEM((rows, 1), F32), pltpu.VMEM((rows, 1), F32), pltpu.VMEM((rows, kw), F32),
                        pltpu.VMEM((PAGE_SIZE, kw), F32), pltpu.VMEM((PAGE_SIZE, kw), F32)],
    )
    return pl.pallas_call(
        functools.partial(_fox_attn_decode_kernel, n_group=n_group),
        grid_spec=grid_spec,
        out_shape=jax.ShapeDtypeStruct((n_b, rows, kw), F32),
        compiler_params=_cparams(("parallel", "arbitrary")),
        name="fox_attn_decode",
    )(page_table, *([k_t] * n_group), *([v_t] * n_group), qbd, k_new, v_new, cq_rows, ck, ck_new)


def _sb_attn_decode_kernel(pt_ref, *refs, n_group):
    k_refs = refs[:n_group]
    v_refs = refs[n_group:2 * n_group]
    qbd_ref, knew_ref, vnew_ref, o_ref, carry_s, acc_s, knew_s, vnew_s = refs[2 * n_group:]
    p = pl.program_id(1)
    qbd = qbd_ref[0].astype(BF16)
    tri2 = _strict_lower_twice(LANES)

    def chunk(kt_bf16, vt_bf16, mask):
        z = jnp.dot(qbd, kt_bf16, preferred_element_type=F32) * ATTN_SCALE
        w_keys = z.shape[1]
        n_sub = w_keys // LANES
        soft = jnp.log1p(jnp.exp(-jnp.abs(z)))
        log_keep = -(jnp.maximum(z, 0.0) + soft)
        log_hit = -(jnp.maximum(-z, 0.0) + soft)
        if mask is not None:
            log_keep = jnp.where(mask, log_keep, 0.0)
        subs = [log_keep[:, s * LANES:(s + 1) * LANES] for s in range(n_sub)]
        sums = [jnp.sum(x, axis=-1, keepdims=True) for x in subs]
        afters = [None] * n_sub
        right = carry_s[...]
        for s in range(n_sub - 1, -1, -1):
            afters[s] = _split_suffix(subs[s], tri2) + right
            right = right + sums[s]
        w = jnp.exp(log_hit + jnp.concatenate(afters, axis=1))
        if mask is not None:
            w = jnp.where(mask, w, 0.0)
        acc_s[...] += _nt_dot(w.astype(BF16), vt_bf16)
        carry_s[...] = right

    @pl.when(p == 0)
    def _():
        carry_s[...] = jnp.zeros_like(carry_s)
        acc_s[...] = jnp.zeros_like(acc_s)
        n_new = knew_ref.shape[1]
        kt = _new_page_t(knew_s, knew_ref[0]).astype(BF16)
        vt = _new_page_t(vnew_s, vnew_ref[0]).astype(BF16)
        t = lax.broadcasted_iota(I32, (qbd.shape[0], PAGE_SIZE), 0) // H_C
        c = lax.broadcasted_iota(I32, (qbd.shape[0], PAGE_SIZE), 1)
        chunk(kt, vt, (c < t) & (c < n_new))

    chunk(_pages(k_refs).astype(BF16), _pages(v_refs).astype(BF16), None)

    @pl.when(p == pl.num_programs(1) - 1)
    def _():
        o_ref[0] = acc_s[...]


def sb_attn_decode(k_t, v_t, layer, page_table, qbd, k_new, v_new, n_group=8):
    n_b, n_pages = page_table.shape
    rows = qbd.shape[1]
    kw = H_C * HEAD_DIM
    steps = n_pages // n_group

    def page_map(g):
        return lambda b, p, pt: (layer, pt[b, (steps - 1 - p) * n_group + g], 0, 0)

    grid_spec = pltpu.PrefetchScalarGridSpec(
        num_scalar_prefetch=1,
        grid=(n_b, steps),
        in_specs=_decode_specs(n_group, kw, page_map, rows, [
            pl.BlockSpec((1,) + k_new.shape[1:], lambda b, p, pt: (b, 0, 0)),
            pl.BlockSpec((1,) + v_new.shape[1:], lambda b, p, pt: (b, 0, 0))]),
        out_specs=pl.BlockSpec((1, rows, kw), lambda b, p, pt: (b, 0, 0)),
        scratch_shapes=[pltpu.VMEM((rows, 1), F32), pltpu.VMEM((rows, kw), F32),
                        pltpu.VMEM((PAGE_SIZE, kw), F32), pltpu.VMEM((PAGE_SIZE, kw), F32)],
    )
    return pl.pallas_call(
        functools.partial(_sb_attn_decode_kernel, n_group=n_group),
        grid_spec=grid_spec,
        out_shape=jax.ShapeDtypeStruct((n_b, rows, kw), F32),
        compiler_params=_cparams(("parallel", "arbitrary")),
        name="sb_attn_decode",
    )(page_table, *([k_t] * n_group), *([v_t] * n_group), qbd, k_new, v_new)


def _block_diag_queries(q, group):
    n_b, n_q, heads, hd = q.shape
    onehot = jax.nn.one_hot(jnp.arange(heads) // group, heads // group, dtype=q.dtype)
    out = q[:, :, :, None, :] * onehot[None, None, :, :, None]
    return out.reshape(n_b, n_q * heads, (heads // group) * hd)


def _take_diag(o, n_q, heads, group):
    n_b = o.shape[0]
    groups = heads // group
    onehot = jax.nn.one_hot(jnp.arange(heads) // group, groups, dtype=o.dtype)
    o = o.reshape(n_b, n_q, heads, groups, HEAD_DIM)
    return jnp.sum(o * onehot[None, None, :, :, None], axis=3).reshape(n_b * n_q, heads * HEAD_DIM)


def _ab_weight(w):
    sizes = (H_A * HEAD_DIM, KV_A * HEAD_DIM, KV_A * HEAD_DIM, IDX_HEADS * IDX_DIM, IDX_DIM, IDX_HEADS,
             H_B * HEAD_DIM, H_B * HEAD_DIM, H_B * HEAD_DIM, H_B)
    offs = [0]
    for sz in sizes:
        offs.append(offs[-1] + sz)
    part = [w[:, offs[i]:offs[i + 1]] for i in range(len(sizes))]
    qa, ka, va, iq, ik, iw, qb, kb, vb, fl = part
    pad = jnp.zeros((w.shape[0], LANES - IDX_DIM - IDX_HEADS - H_B), w.dtype)
    return jnp.concatenate([qa, ka, va, iq, ik, iw, fl, pad, qb, kb, vb], axis=1).astype(BF16)


def kernel(x_prompt, x_sample, cache_a_k, cache_a_v, cache_a_idxk, cache_b_k, cache_b_v, cache_b_logf, cache_c_k, cache_c_v, cache_mem_k, cache_mem_v, page_table, mem_prompt, w_in_ab, b_forget, w_out_ab, w_in_c, w_out_c, rel_bias, w_xq, w_xkv, w_xo, ln_g, ln_b, w_router, b_router, w_gate_up, b_gate_up, w_down, b_down):
    n_b, t, d = x_prompt.shape
    s_b, s_t, _ = x_sample.shape
    n_p = n_b * t
    n_s = s_b * s_t
    n_mem = mem_prompt.shape[1]
    past = page_table.shape[1] * PAGE_SIZE
    kvw = KV_A * HEAD_DIM
    xw = XA_HEADS * XA_HEAD_DIM

    x = jnp.concatenate([x_prompt.reshape(n_p, d), x_sample.reshape(n_s, d)], axis=0)
    win = bias_window(rel_bias)
    far = rel_bias[N_BUCKETS - 1]
    brow = bias_rows(rel_bias, past, s_t, past + PAGE_SIZE)
    mem_p = mem_prompt.reshape(n_b * n_mem, d)

    def pages_t(c):
        ct = jnp.moveaxis(c, 2, -1)
        return ct.reshape(ct.shape[0], ct.shape[1], -1, PAGE_SIZE)

    ak_t, av_t, ik_t = pages_t(cache_a_k), pages_t(cache_a_v), pages_t(cache_a_idxk)
    bk_t, bv_t, lf_t = pages_t(cache_b_k), pages_t(cache_b_v), pages_t(cache_b_logf)
    ck_t, cv_t = pages_t(cache_c_k), pages_t(cache_c_v)

    rows_ab_p, rows_ab_s, rows_c_p, rows_c_s, mem_kv = [], [], [], [], []
    for li in range(DEPTH):
        j = li // 2
        if li % 2 == 0:
            bf_row = jnp.zeros((1, LANES), F32).at[0, MISC_FL:MISC_FL + H_B].set(b_forget[j])
            qa, kva, iq, misc, qb, kb, vb = ab_proj(x, _ab_weight(w_in_ab[j]), bf_row)
            o_a_p = dsa_prompt(qa, iq, misc, kva, win, far, n_b, t)
            cum, cumt = cumsum_prompt(misc, n_b, t)
            o_b_p = fox_prompt(qb, kb, vb, cum, cumt, n_b, t)
            misc_s = misc[n_p:].reshape(s_b, s_t, LANES)
            ik_new = misc_s[:, :, MISC_IK:MISC_IK + IDX_DIM]
            iw_rows = misc_s[:, :, MISC_IW:MISC_IW + IDX_HEADS].reshape(s_b, s_t * IDX_HEADS, 1)
            logf_new = misc_s[:, :, MISC_FL:MISC_FL + H_B]
            iq_rows = iq[n_p:].reshape(s_b, s_t * IDX_HEADS, IDX_DIM)
            kva_s = kva[n_p:].reshape(s_b, s_t, 2 * kvw)
            sc_past, sc_new = dsa_scores_decode(ik_t, j, page_table, iq_rows, iw_rows, ik_new)
            scores = jnp.concatenate([sc_past, sc_new], axis=-1).reshape(n_s, past + PAGE_SIZE)
            sel = topk_select(scores, min(TOPK_MAX, (past + s_t) // 4)).reshape(s_b, s_t, past + PAGE_SIZE)
            qbd_a = _block_diag_queries(qa[n_p:].reshape(s_b, s_t, H_A, HEAD_DIM), H_A // KV_A)
            o_a_s = _take_diag(dsa_attn_decode(ak_t, av_t, j, page_table, qbd_a, kva_s, sel, brow),
                               s_t, H_A, H_A // KV_A)
            ck, ck_new = cumsum_pages(lf_t, j, page_table, logf_new)
            cq_rows = jnp.swapaxes(ck_new[:, :, :s_t], 1, 2).reshape(s_b, s_t * H_B, 1)
            qbd_b = _block_diag_queries(qb[n_p:].reshape(s_b, s_t, H_B, HEAD_DIM), 1)
            kb_s = kb[n_p:].reshape(s_b, s_t, H_B * HEAD_DIM)
            vb_s = vb[n_p:].reshape(s_b, s_t, H_B * HEAD_DIM)
            o_b_s = _take_diag(fox_attn_decode(bk_t, bv_t, j, page_table, qbd_b, kb_s, vb_s,
                                               cq_rows, ck, ck_new), s_t, H_B, 1)
            o_a = jnp.concatenate([o_a_p, o_a_s], axis=0)
            o_b = jnp.concatenate([o_b_p, o_b_s], axis=0)
            w_out = w_out_ab[j].astype(BF16)
            x = linear2_res_ln(o_a, w_out[:H_A * HEAD_DIM], o_b, w_out[H_A * HEAD_DIM:], x,
                               ln_g[li, 0], ln_b[li, 0])
            rows = (kva[:, :kvw], kva[:, kvw:], misc[:, MISC_IK:MISC_IK + IDX_DIM], kb, vb,
                    misc[:, MISC_FL:MISC_FL + H_B])
            tails = ((KV_A, HEAD_DIM), (KV_A, HEAD_DIM), (IDX_DIM,), (H_B, HEAD_DIM), (H_B, HEAD_DIM), (H_B,))
            rows_ab_p.append(tuple(r[:n_p].reshape(n_b, t, *tl) for r, tl in zip(rows, tails)))
            rows_ab_s.append(tuple(r[n_p:].reshape(s_b, s_t, *tl) for r, tl in zip(rows, tails)))
        else:
            q, k, v = linear(x, w_in_c[j].astype(BF16), (H_C * HEAD_DIM,) * 3, name="c_proj")
            o_p = sb_prompt(q, k, v, n_b, t)
            qbd = _block_diag_queries(q[n_p:].reshape(s_b, s_t, H_C, HEAD_DIM), 1)
            k_s = k[n_p:].reshape(s_b, s_t, H_C * HEAD_DIM)
            v_s = v[n_p:].reshape(s_b, s_t, H_C * HEAD_DIM)
            o_s = _take_diag(sb_attn_decode(ck_t, cv_t, j, page_table, qbd, k_s, v_s), s_t, H_C, 1)
            o = jnp.concatenate([o_p, o_s], axis=0)
            x = linear_res_ln(o, w_out_c[j].astype(BF16), x, ln_g[li, 0], ln_b[li, 0], name="c_out")
            rows_c_p.append((k[:n_p].reshape(n_b, t, H_C, HEAD_DIM), v[:n_p].reshape(n_b, t, H_C, HEAD_DIM)))
            rows_c_s.append((k_s.reshape(s_b, s_t, H_C, HEAD_DIM), v_s.reshape(s_b, s_t, H_C, HEAD_DIM)))

        mk_p, mv_p = linear(mem_p, w_xkv[li].astype(BF16), (xw, xw), tm=256, name="mem_proj")
        mem_kv.append((mk_p, mv_p))
        (xq,) = linear(x, w_xq[li].astype(BF16), (xw,), name="xq_proj")
        o = xattn(xq, mk_p.reshape(n_b, n_mem, xw), mv_p.reshape(n_b, n_mem, xw), 0, t, 1)
        o = xattn(xq, cache_mem_k[li].reshape(s_b, n_mem, xw), cache_mem_v[li].reshape(s_b, n_mem, xw),
                  n_p, s_t, 8, prev=o)
        x = linear_res_ln(o, w_xo[li].astype(BF16), x, ln_g[li, 1], ln_b[li, 1], name="xo_proj")

        x = moe_ln(x, w_router, b_router, w_gate_up, b_gate_up, w_down, b_down, li, ln_g[li, 2], ln_b[li, 2])

    def stack(rows):
        return tuple(jnp.stack([r[i] for r in rows]) for i in range(len(rows[0])))

    p_mem_k = jnp.stack([kv[0].reshape(n_b, n_mem, XA_HEADS, XA_HEAD_DIM) for kv in mem_kv])
    p_mem_v = jnp.stack([kv[1].reshape(n_b, n_mem, XA_HEADS, XA_HEAD_DIM) for kv in mem_kv])
    return ((x[:n_p].reshape(n_b, t, d), x[n_p:].reshape(s_b, s_t, d))
            + stack(rows_ab_p) + stack(rows_c_p) + (p_mem_k, p_mem_v) + stack(rows_ab_s) + stack(rows_c_s))
```

```python
import functools
import math

import jax
import jax.numpy as jnp
from jax import lax
from jax.experimental import pallas as pl
from jax.experimental.pallas import tpu as pltpu

F32 = jnp.float32
BF16 = jnp.bfloat16
I32 = jnp.int32

D_MODEL = 1024
DEPTH = 4
PAGE_SIZE = 128
HEAD_DIM = 64
H_A = 8
KV_A = 2
IDX_HEADS = 8
IDX_DIM = 64
TOPK_MAX = 256
H_B = 8
H_C = 16
N_BUCKETS = 32
MAX_DISTANCE = 128
XA_HEADS = 4
XA_HEAD_DIM = 128
N_EXPERTS = 32
TOP_K = 4
D_EXPERT = D_MODEL
SWIGLU_LIMIT = 7.0
SWIGLU_ALPHA = 1.702
LN_EPS = 1e-5
NEG_INF = -1e30
ATTN_SCALE = HEAD_DIM ** -0.5
DN_ALPHA = (2 * DEPTH) ** 0.25

LANES = 128
Q_TILE = 128
ROW_TILE = 384
MOE_TILE = 256
VMEM_LIMIT = 56 * 1024 * 1024

MISC_IK = 0
MISC_IW = IDX_DIM
MISC_FL = IDX_DIM + IDX_HEADS

_HI = lax.Precision.HIGHEST


def _cparams(sem):
    return pltpu.CompilerParams(dimension_semantics=sem, vmem_limit_bytes=VMEM_LIMIT)


def _log_sigmoid(x):
    return -(jnp.maximum(-x, 0.0) + jnp.log1p(jnp.exp(-jnp.abs(x))))


def _nt_dot(a, b):
    return lax.dot_general(a, b, (((1,), (1,)), ((), ())), preferred_element_type=F32)


def _linear_kernel(x_ref, w_ref, *o_refs):
    xb = x_ref[...].astype(BF16)
    off = 0
    for o in o_refs:
        n = o.shape[-1]
        o[...] = jnp.dot(xb, w_ref[:, off:off + n], preferred_element_type=F32)
        off += n


def linear(x, w_bf16, splits, tm=ROW_TILE, name="linear"):
    m, k = x.shape
    n = w_bf16.shape[1]
    assert sum(splits) == n and all(s % LANES == 0 for s in splits) and m % tm == 0
    return pl.pallas_call(
        _linear_kernel,
        grid=(m // tm,),
        in_specs=[pl.BlockSpec((tm, k), lambda i: (i, 0)),
                  pl.BlockSpec((k, n), lambda i: (0, 0))],
        out_specs=[pl.BlockSpec((tm, s), lambda i: (i, 0)) for s in splits],
        out_shape=[jax.ShapeDtypeStruct((m, s), F32) for s in splits],
        compiler_params=_cparams(("parallel",)),
        name=name,
    )(x, w_bf16)


def _ab_proj_kernel(x_ref, w_ref, bf_ref, qa_ref, kva_ref, iq_ref, misc_ref, qb_ref, kb_ref, vb_ref):
    xb = x_ref[...].astype(BF16)
    off = 0
    for o in (qa_ref, kva_ref, iq_ref, misc_ref, qb_ref, kb_ref, vb_ref):
        n = o.shape[-1]
        r = jnp.dot(xb, w_ref[:, off:off + n], preferred_element_type=F32)
        if o is misc_ref:
            lane = lax.broadcasted_iota(I32, r.shape, 1)
            is_fl = (lane >= MISC_FL) & (lane < MISC_FL + H_B)
            r = jnp.where(is_fl, _log_sigmoid(r + bf_ref[...]), r)
        o[...] = r
        off += n


def ab_proj(x, w_bf16, bf_row, tm=ROW_TILE):
    m, k = x.shape
    splits = (H_A * HEAD_DIM, 2 * KV_A * HEAD_DIM, IDX_HEADS * IDX_DIM, LANES,
              H_B * HEAD_DIM, H_B * HEAD_DIM, H_B * HEAD_DIM)
    n = sum(splits)
    return pl.pallas_call(
        _ab_proj_kernel,
        grid=(m // tm,),
        in_specs=[pl.BlockSpec((tm, k), lambda i: (i, 0)),
                  pl.BlockSpec((k, n), lambda i: (0, 0)),
                  pl.BlockSpec((1, LANES), lambda i: (0, 0))],
        out_specs=[pl.BlockSpec((tm, s), lambda i: (i, 0)) for s in splits],
        out_shape=[jax.ShapeDtypeStruct((m, s), F32) for s in splits],
        compiler_params=_cparams(("parallel",)),
        name="ab_proj",
    )(x, w_bf16, bf_row)


def _layer_norm(z, g, b):
    mu = jnp.mean(z, axis=-1, keepdims=True)
    zc = z - mu
    var = jnp.mean(zc * zc, axis=-1, keepdims=True)
    return zc * lax.rsqrt(var + LN_EPS) * g + b


def _linear_res_ln_kernel(h_ref, w_ref, x_ref, g_ref, b_ref, o_ref):
    y = jnp.dot(h_ref[...].astype(BF16), w_ref[...], preferred_element_type=F32)
    o_ref[...] = _layer_norm(DN_ALPHA * x_ref[...] + y, g_ref[...], b_ref[...])


def linear_res_ln(h, w_bf16, x, g, b, tm=ROW_TILE, name="linear_res_ln"):
    m, k = h.shape
    d = w_bf16.shape[1]
    return pl.pallas_call(
        _linear_res_ln_kernel,
        grid=(m // tm,),
        in_specs=[pl.BlockSpec((tm, k), lambda i: (i, 0)),
                  pl.BlockSpec((k, d), lambda i: (0, 0)),
                  pl.BlockSpec((tm, d), lambda i: (i, 0)),
                  pl.BlockSpec((1, d), lambda i: (0, 0)),
                  pl.BlockSpec((1, d), lambda i: (0, 0))],
        out_specs=pl.BlockSpec((tm, d), lambda i: (i, 0)),
        out_shape=jax.ShapeDtypeStruct((m, d), F32),
        compiler_params=_cparams(("parallel",)),
        name=name,
    )(h, w_bf16, x, g.reshape(1, d), b.reshape(1, d))


def _linear2_res_ln_kernel(h1_ref, w1_ref, h2_ref, w2_ref, x_ref, g_ref, b_ref, o_ref):
    y = jnp.dot(h1_ref[...].astype(BF16), w1_ref[...], preferred_element_type=F32)
    y = y + jnp.dot(h2_ref[...].astype(BF16), w2_ref[...], preferred_element_type=F32)
    o_ref[...] = _layer_norm(DN_ALPHA * x_ref[...] + y, g_ref[...], b_ref[...])


def linear2_res_ln(h1, w1, h2, w2, x, g, b, tm=ROW_TILE):
    m, k1 = h1.shape
    k2 = h2.shape[1]
    d = w1.shape[1]
    return pl.pallas_call(
        _linear2_res_ln_kernel,
        grid=(m // tm,),
        in_specs=[pl.BlockSpec((tm, k1), lambda i: (i, 0)),
                  pl.BlockSpec((k1, d), lambda i: (0, 0)),
                  pl.BlockSpec((tm, k2), lambda i: (i, 0)),
                  pl.BlockSpec((k2, d), lambda i: (0, 0)),
                  pl.BlockSpec((tm, d), lambda i: (i, 0)),
                  pl.BlockSpec((1, d), lambda i: (0, 0)),
                  pl.BlockSpec((1, d), lambda i: (0, 0))],
        out_specs=pl.BlockSpec((tm, d), lambda i: (i, 0)),
        out_shape=jax.ShapeDtypeStruct((m, d), F32),
        compiler_params=_cparams(("parallel",)),
        name="ab_out",
    )(h1, w1, h2, w2, x, g.reshape(1, d), b.reshape(1, d))


def _t5_bucket(rel):
    n = jnp.maximum(rel, 0)
    max_exact = N_BUCKETS // 2
    nf = jnp.maximum(n, 1).astype(F32)
    large = max_exact + (jnp.log(nf / max_exact) / math.log(MAX_DISTANCE / max_exact)
                         * (N_BUCKETS - max_exact)).astype(I32)
    large = jnp.minimum(large, N_BUCKETS - 1)
    return jnp.where(n < max_exact, n, large)


def _bias_lookup(bucket, tab_col):
    out = jnp.zeros(bucket.shape, F32)
    for b in range(N_BUCKETS):
        out = jnp.where(bucket == b, tab_col[:, b:b + 1], out)
    return out


def _bias_window_kernel(tab_ref, o_ref):
    r = lax.broadcasted_iota(I32, (Q_TILE, 2 * Q_TILE), 0)
    c = lax.broadcasted_iota(I32, (Q_TILE, 2 * Q_TILE), 1)
    bucket = _t5_bucket(r + Q_TILE - c)
    tab_col = jnp.broadcast_to(tab_ref[0], (Q_TILE, N_BUCKETS))
    o_ref[0] = _bias_lookup(bucket, tab_col)


def bias_window(rel_bias):
    tab_t = rel_bias.T.reshape(H_A, 1, N_BUCKETS)
    return pl.pallas_call(
        _bias_window_kernel,
        grid=(H_A,),
        in_specs=[pl.BlockSpec((1, 1, N_BUCKETS), lambda h: (h, 0, 0))],
        out_specs=pl.BlockSpec((1, Q_TILE, 2 * Q_TILE), lambda h: (h, 0, 0)),
        out_shape=jax.ShapeDtypeStruct((H_A, Q_TILE, 2 * Q_TILE), F32),
        compiler_params=_cparams(("parallel",)),
        name="bias_window",
    )(tab_t)


def _bias_rows_kernel(tab_ref, o_ref, *, past, n_q):
    rows, cols = o_ref.shape
    t = lax.broadcasted_iota(I32, (rows, cols), 0) // H_A
    s = lax.broadcasted_iota(I32, (rows, cols), 1)
    o_ref[...] = _bias_lookup(_t5_bucket(past + t - s), tab_ref[...])


def bias_rows(rel_bias, past, n_q, n_cols):
    tab_rows = jnp.tile(rel_bias.T, (n_q, 1))
    return pl.pallas_call(
        functools.partial(_bias_rows_kernel, past=past, n_q=n_q),
        out_shape=jax.ShapeDtypeStruct((n_q * H_A, n_cols), F32),
        compiler_params=pltpu.CompilerParams(vmem_limit_bytes=VMEM_LIMIT),
        name="bias_rows",
    )(tab_rows)


def _sort_key(score):
    bits = pltpu.bitcast(score + 0.0, I32)
    return bits ^ ((bits >> 31) & 0x7FFFFFFF)


def _lane_total(acc):
    return jnp.dot(acc.astype(BF16), jnp.ones((LANES, LANES), BF16), preferred_element_type=F32)


def _count_tiles(pred, n_tiles, rows):
    acc = jnp.zeros((rows, LANES), F32)
    for j in range(n_tiles):
        acc = acc + jnp.where(pred(j), 1.0, 0.0)
    return _lane_total(acc)


def _topk_threshold(key, k, cut_ref):
    r, n = key.shape
    n_tiles = n // LANES
    assert n % LANES == 0 and n_tiles <= 256 and r % 16 == 0
    kf = float(k)
    hr = r // 2
    halves = [[key[h * hr:(h + 1) * hr, j * LANES:(j + 1) * LANES] for j in range(n_tiles)] for h in range(2)]

    def body(it, ans):
        bit = jnp.left_shift(jnp.int32(1), 31 - it)
        out = []
        for h in range(2):
            cand = ans[h] + bit
            cnt = _count_tiles(lambda j: halves[h][j] >= cand, n_tiles, hr)
            out.append(jnp.where(cnt >= kf, cand, ans[h]))
        return tuple(out)

    init = jnp.full((hr, LANES), -2 ** 31, I32)
    thr_rep = lax.fori_loop(0, 32, body, (init, init))
    need, n_eq = [], []
    for h in range(2):
        need.append(kf - _count_tiles(lambda j: halves[h][j] > thr_rep[h], n_tiles, hr))
        n_eq.append(_count_tiles(lambda j: halves[h][j] == thr_rep[h], n_tiles, hr))
    cut_ref[...] = jnp.full((r, 1), n, I32)
    excess = jnp.maximum(jnp.max(n_eq[0] - need[0]), jnp.max(n_eq[1] - need[1]))

    @pl.when(excess > 0.0)
    def _():
        lane = lax.broadcasted_iota(I32, (hr, LANES), 1)
        n_bits = max(1, (n - 1).bit_length())

        def cbody(it, c):
            bit = jnp.left_shift(jnp.int32(1), n_bits - 1 - it)
            out = []
            for h in range(2):
                cand = c[h] + bit
                cnt = _count_tiles(
                    lambda j: (halves[h][j] == thr_rep[h]) & (lane + j * LANES < cand), n_tiles, hr)
                out.append(jnp.where(cnt < need[h], cand, c[h]))
            return tuple(out)

        zero = jnp.zeros((hr, LANES), I32)
        cut = lax.fori_loop(0, n_bits, cbody, (zero, zero))
        cut_ref[0:hr, :] = cut[0][:, 0:1]
        cut_ref[hr:r, :] = cut[1][:, 0:1]

    return jnp.concatenate([thr_rep[0][:, 0:1], thr_rep[1][:, 0:1]], axis=0)


def _dsa_prompt_kernel(far_ref, qa_ref, iq_ref, mq_ref, kva_ref, mk_ref, win_ref, o_ref,
                       s_ref, cut_ref, *, n_sel, q_lo):
    i = q_lo + pl.program_id(1)
    tq, t_keys = s_ref.shape
    qpos = i * tq + lax.broadcasted_iota(I32, (tq, t_keys), 0)
    kpos = lax.broadcasted_iota(I32, (tq, t_keys), 1)
    causal = kpos <= qpos

    ikb = mk_ref[:, MISC_IK:MISC_IK + IDX_DIM].astype(BF16)
    score = jnp.zeros((tq, t_keys), F32)
    for n in range(IDX_HEADS):
        iqn = iq_ref[:, n * IDX_DIM:(n + 1) * IDX_DIM].astype(BF16)
        dots = _nt_dot(iqn, ikb)
        score = score + jnp.maximum(dots, 0.0) * mq_ref[:, MISC_IW + n:MISC_IW + n + 1]
    score = jnp.where(causal, score, -jnp.inf)
    key = _sort_key(score)
    thr = _topk_threshold(key, n_sel, cut_ref)
    sel = ((key > thr) | ((key == thr) & (kpos <= cut_ref[...]))) & causal

    off = pl.multiple_of(jnp.maximum(i - 1, 0) * tq, tq)
    rep = H_A // KV_A
    for h in range(H_A):
        g = h // rep
        qh = (qa_ref[:, h * HEAD_DIM:(h + 1) * HEAD_DIM] * ATTN_SCALE).astype(BF16)
        kg = kva_ref[:, g * HEAD_DIM:(g + 1) * HEAD_DIM].astype(BF16)
        vg = kva_ref[:, (KV_A + g) * HEAD_DIM:(KV_A + g + 1) * HEAD_DIM].astype(BF16)
        s_ref[...] = _nt_dot(qh, kg)
        w = win_ref[h] - far_ref[h]
        w = jnp.where(i == 0, jnp.concatenate([w[:, tq:], w[:, :tq]], axis=1), w)
        s_ref[:, pl.ds(off, 2 * tq)] += w
        logits = jnp.where(sel, s_ref[...], NEG_INF)
        m = jnp.max(logits, axis=-1, keepdims=True)
        p = jnp.exp(logits - m)
        o = jnp.dot(p.astype(BF16), vg, preferred_element_type=F32)
        o_ref[0, :, h * HEAD_DIM:(h + 1) * HEAD_DIM] = o / jnp.sum(p, axis=-1, keepdims=True)


def _causal_spans(n_blocks, tq, t):
    if t // 4 < 2 * tq or n_blocks % 4:
        return [(0, n_blocks, t)]
    quarter = n_blocks // 4
    return [(0, quarter, t // 4), (quarter, quarter, t // 2), (2 * quarter, 2 * quarter, t)]


def dsa_prompt(qa, iq, misc, kva, win, far, n_batch, t):
    nq = t // Q_TILE
    n_sel = min(TOPK_MAX, t // 4)
    outs = []
    for q_lo, n_qb, t_keys in _causal_spans(nq, Q_TILE, t):
        kb = t // t_keys

        def qmap(b, i, far, q_lo=q_lo):
            return (b * nq + q_lo + i, 0)

        grid_spec = pltpu.PrefetchScalarGridSpec(
            num_scalar_prefetch=1,
            grid=(n_batch, n_qb),
            in_specs=[pl.BlockSpec((Q_TILE, H_A * HEAD_DIM), qmap),
                      pl.BlockSpec((Q_TILE, IDX_HEADS * IDX_DIM), qmap),
                      pl.BlockSpec((Q_TILE, LANES), qmap),
                      pl.BlockSpec((t_keys, 2 * KV_A * HEAD_DIM), lambda b, i, far, kb=kb: (b * kb, 0)),
                      pl.BlockSpec((t_keys, LANES), lambda b, i, far, kb=kb: (b * kb, 0)),
                      pl.BlockSpec((H_A, Q_TILE, 2 * Q_TILE), lambda b, i, far: (0, 0, 0))],
            out_specs=pl.BlockSpec((1, Q_TILE, H_A * HEAD_DIM), lambda b, i, far: (b, i, 0)),
            scratch_shapes=[pltpu.VMEM((Q_TILE, t_keys), F32), pltpu.VMEM((Q_TILE, 1), I32)],
        )
        outs.append(pl.pallas_call(
            functools.partial(_dsa_prompt_kernel, n_sel=n_sel, q_lo=q_lo),
            grid_spec=grid_spec,
            out_shape=jax.ShapeDtypeStruct((n_batch, n_qb * Q_TILE, H_A * HEAD_DIM), F32),
            compiler_params=_cparams(("parallel", "parallel")),
            name="dsa_prompt",
        )(far, qa, iq, misc, kva, misc, win))
    return jnp.concatenate(outs, axis=1).reshape(n_batch * t, H_A * HEAD_DIM)


def _fox_prompt_kernel(q_ref, k_ref, v_ref, cum_ref, cumt_ref, o_ref, *, q_lo):
    i = q_lo + pl.program_id(2)
    hp = pl.program_id(1)
    tq = q_ref.shape[0]
    t_keys = k_ref.shape[0]
    qpos = i * tq + lax.broadcasted_iota(I32, (tq, t_keys), 0)
    kpos = lax.broadcasted_iota(I32, (tq, t_keys), 1)
    causal = kpos <= qpos
    lane = lax.broadcasted_iota(I32, cum_ref.shape, 1)
    for hh in range(2):
        sl = slice(hh * HEAD_DIM, (hh + 1) * HEAD_DIM)
        head = 2 * hp + hh
        qh = (q_ref[:, sl] * ATTN_SCALE).astype(BF16)
        kh = k_ref[:, sl].astype(BF16)
        vh = v_ref[:, sl].astype(BF16)
        cq = jnp.sum(jnp.where(lane == MISC_FL + head, cum_ref[...], 0.0), axis=-1, keepdims=True)
        ck = cumt_ref[0, pl.ds(head, 1), :]
        logits = _nt_dot(qh, kh) + cq - ck
        logits = jnp.where(causal, logits, NEG_INF)
        m = jnp.max(logits, axis=-1, keepdims=True)
        p = jnp.exp(logits - m)
        o = jnp.dot(p.astype(BF16), vh, preferred_element_type=F32)
        o_ref[0, :, sl] = o / jnp.sum(p, axis=-1, keepdims=True)


def fox_prompt(qb, kb, vb, cum, cumt, n_batch, t, tq=2 * Q_TILE):
    assert MISC_FL % 8 == 0
    nq = t // tq
    hw = 2 * HEAD_DIM
    outs = []
    for q_lo, n_qb, t_keys in _causal_spans(nq, tq, t):
        kbl = t // t_keys

        def qmap(b, h, i, q_lo=q_lo):
            return (b * nq + q_lo + i, h)

        outs.append(pl.pallas_call(
            functools.partial(_fox_prompt_kernel, q_lo=q_lo),
            grid=(n_batch, H_B // 2, n_qb),
            in_specs=[pl.BlockSpec((tq, hw), qmap),
                      pl.BlockSpec((t_keys, hw), lambda b, h, i, kbl=kbl: (b * kbl, h)),
                      pl.BlockSpec((t_keys, hw), lambda b, h, i, kbl=kbl: (b * kbl, h)),
                      pl.BlockSpec((tq, LANES), lambda b, h, i, q_lo=q_lo: (b * nq + q_lo + i, 0)),
                      pl.BlockSpec((1, H_B, t_keys), lambda b, h, i: (b, MISC_FL // H_B, 0))],
            out_specs=pl.BlockSpec((1, tq, hw), lambda b, h, i: (b, i, h)),
            out_shape=jax.ShapeDtypeStruct((n_batch, n_qb * tq, H_B * HEAD_DIM), F32),
            compiler_params=_cparams(("parallel", "parallel", "parallel")),
            name="fox_prompt",
        )(qb, kb, vb, cum, cumt))
    return jnp.concatenate(outs, axis=1).reshape(n_batch * t, H_B * HEAD_DIM)


def _split_suffix(lk, tri2):
    hi = lk.astype(BF16)
    lo = (lk - hi.astype(F32)).astype(BF16)
    return jnp.dot(jnp.concatenate([hi, lo], axis=1), tri2, preferred_element_type=F32)


def _strict_lower_twice(n):
    j = lax.broadcasted_iota(I32, (2 * n, n), 0)
    s = lax.broadcasted_iota(I32, (2 * n, n), 1)
    j = jnp.where(j >= n, j - n, j)
    return jnp.where(j > s, 1.0, 0.0).astype(BF16)


def _stick_chunk(z, v_bf16, carry, acc, tri2, mask):
    w_keys = z.shape[1]
    n_sub = w_keys // LANES
    log_keep = -(jnp.maximum(z, 0.0) + jnp.log(1.0 + jnp.exp(-jnp.abs(z))))
    log_hit = z + log_keep
    if mask is not None:
        log_keep = jnp.where(mask, log_keep, 0.0)
    subs = [log_keep[:, s * LANES:(s + 1) * LANES] for s in range(n_sub)]
    sums = [jnp.sum(x, axis=-1, keepdims=True) for x in subs]
    afters = [None] * n_sub
    right = carry
    for s in range(n_sub - 1, -1, -1):
        afters[s] = _split_suffix(subs[s], tri2) + right
        right = right + sums[s]
    w = jnp.exp(log_hit + jnp.concatenate(afters, axis=1))
    if mask is not None:
        w = jnp.where(mask, w, 0.0)
    acc = acc + jnp.dot(w.astype(BF16), v_bf16, preferred_element_type=F32)
    return right, acc


def _sb_prompt_kernel(q_ref, k_ref, v_ref, o_ref, *, chunk):
    i = pl.program_id(2)
    tq = q_ref.shape[0]
    tri2 = _strict_lower_twice(LANES)
    qs = [(q_ref[:, hh * HEAD_DIM:(hh + 1) * HEAD_DIM] * ATTN_SCALE).astype(BF16) for hh in range(2)]

    def chunk_step(c, st, masked):
        base = pl.multiple_of(c * chunk, chunk)
        mask = None
        if masked:
            qpos = i * tq + lax.broadcasted_iota(I32, (tq, chunk), 0)
            kpos = base + lax.broadcasted_iota(I32, (tq, chunk), 1)
            mask = kpos < qpos
        out = []
        for hh in range(2):
            cs = slice(hh * HEAD_DIM, (hh + 1) * HEAD_DIM)
            kh = k_ref[pl.ds(base, chunk), cs].astype(BF16)
            vh = v_ref[pl.ds(base, chunk), cs].astype(BF16)
            z = _nt_dot(qs[hh], kh)
            out.extend(_stick_chunk(z, vh, st[2 * hh], st[2 * hh + 1], tri2, mask))
        return tuple(out)

    c_diag = (i * tq) // chunk
    zero = (jnp.zeros((tq, 1), F32), jnp.zeros((tq, HEAD_DIM), F32))
    st = chunk_step(c_diag, zero + zero, True)
    st = lax.fori_loop(0, c_diag, lambda step, s_: chunk_step(c_diag - 1 - step, s_, False), st)
    o_ref[:, :HEAD_DIM] = st[1]
    o_ref[:, HEAD_DIM:] = st[3]


def sb_prompt(q, k, v, n_batch, t):
    nq = t // Q_TILE
    hw = 2 * HEAD_DIM
    chunk = min(4 * LANES, t)
    return pl.pallas_call(
        functools.partial(_sb_prompt_kernel, chunk=chunk),
        grid=(n_batch, H_C // 2, nq),
        in_specs=[pl.BlockSpec((Q_TILE, hw), lambda b, h, i: (b * nq + i, h)),
                  pl.BlockSpec((t, hw), lambda b, h, i: (b, h)),
                  pl.BlockSpec((t, hw), lambda b, h, i: (b, h))],
        out_specs=pl.BlockSpec((Q_TILE, hw), lambda b, h, i: (b * nq + i, h)),
        out_shape=jax.ShapeDtypeStruct((n_batch * t, H_C * HEAD_DIM), F32),
        compiler_params=_cparams(("parallel", "parallel", "parallel")),
        name="sb_prompt",
    )(q, k, v)


def _cumsum_prompt_kernel(m_ref, cum_ref, cumt_ref, carry_ref):
    i = pl.program_id(1)
    tb = m_ref.shape[0]

    @pl.when(i == 0)
    def _():
        carry_ref[...] = jnp.zeros_like(carry_ref)

    r = lax.broadcasted_iota(I32, (tb, tb), 0)
    c = lax.broadcasted_iota(I32, (tb, tb), 1)
    lower = jnp.where(c <= r, 1.0, 0.0)
    cum = jnp.dot(lower, m_ref[...], precision=_HI, preferred_element_type=F32) + carry_ref[...]
    cum_ref[...] = cum
    carry_ref[...] = cum[tb - 1:tb, :]
    eye = jnp.where(lax.broadcasted_iota(I32, (LANES, LANES), 0) == lax.broadcasted_iota(I32, (LANES, LANES), 1),
                    1.0, 0.0)
    cumt_ref[0] = lax.dot_general(eye, cum, (((1,), (1,)), ((), ())), precision=_HI,
                                  preferred_element_type=F32)


def cumsum_prompt(misc, n_batch, t, tb=256):
    nb = t // tb
    return pl.pallas_call(
        _cumsum_prompt_kernel,
        grid=(n_batch, nb),
        in_specs=[pl.BlockSpec((tb, LANES), lambda b, i: (b * nb + i, 0))],
        out_specs=[pl.BlockSpec((tb, LANES), lambda b, i: (b * nb + i, 0)),
                   pl.BlockSpec((1, LANES, tb), lambda b, i: (b, 0, i))],
        out_shape=[jax.ShapeDtypeStruct((n_batch * t, LANES), F32),
                   jax.ShapeDtypeStruct((n_batch, LANES, t), F32)],
        scratch_shapes=[pltpu.VMEM((1, LANES), F32)],
        compiler_params=_cparams(("parallel", "arbitrary")),
        name="cumsum_prompt",
    )(misc)


def _cumsum_pages_kernel(pt_ref, *refs, n_group):
    page_refs = refs[:n_group]
    new_ref, cum_ref, cumnew_ref, carry_ref = refs[n_group:]
    p = pl.program_id(1)

    @pl.when(p == 0)
    def _():
        carry_ref[...] = jnp.zeros_like(carry_ref)

    r = lax.broadcasted_iota(I32, (PAGE_SIZE, PAGE_SIZE), 0)
    c = lax.broadcasted_iota(I32, (PAGE_SIZE, PAGE_SIZE), 1)
    upper = jnp.where(r <= c, 1.0, 0.0)
    carry = carry_ref[...]
    within = [jnp.dot(r_[...], upper, precision=_HI, preferred_element_type=F32) for r_ in page_refs]
    for g in range(n_group):
        cum = within[g] + carry
        cum_ref[0, :, g * PAGE_SIZE:(g + 1) * PAGE_SIZE] = cum
        carry = cum[:, PAGE_SIZE - 1:PAGE_SIZE]
    carry_ref[...] = carry

    @pl.when(p == pl.num_programs(1) - 1)
    def _():
        n_new = new_ref.shape[1]
        eye = jnp.where(lax.broadcasted_iota(I32, (H_B, H_B), 0) == lax.broadcasted_iota(I32, (H_B, H_B), 1),
                        1.0, 0.0)
        newt = lax.dot_general(eye, new_ref[0], (((1,), (1,)), ((), ())), precision=_HI,
                               preferred_element_type=F32)
        lane = lax.broadcasted_iota(I32, (H_B, LANES), 1)
        out = jnp.zeros((H_B, LANES), F32)
        run = carry
        for s in range(n_new):
            run = run + newt[:, s:s + 1]
            out = jnp.where(lane == s, run, out)
        cumnew_ref[0] = out


def cumsum_pages(logf_t, layer, page_table, logf_new, n_group=32):
    n_b, n_pages = page_table.shape
    n_group = math.gcd(n_group, n_pages)
    steps = n_pages // n_group

    def page_map(g):
        return lambda b, p, pt: (layer, pt[b, p * n_group + g], 0, 0)

    grid_spec = pltpu.PrefetchScalarGridSpec(
        num_scalar_prefetch=1,
        grid=(n_b, steps),
        in_specs=[pl.BlockSpec((None, None, H_B, PAGE_SIZE), page_map(g)) for g in range(n_group)]
        + [pl.BlockSpec((1, logf_new.shape[1], H_B), lambda b, p, pt: (b, 0, 0))],
        out_specs=[pl.BlockSpec((1, H_B, n_group * PAGE_SIZE), lambda b, p, pt: (b, 0, p)),
                   pl.BlockSpec((1, H_B, LANES), lambda b, p, pt: (b, 0, 0))],
        scratch_shapes=[pltpu.VMEM((H_B, 1), F32)],
    )
    return pl.pallas_call(
        functools.partial(_cumsum_pages_kernel, n_group=n_group),
        grid_spec=grid_spec,
        out_shape=[jax.ShapeDtypeStruct((n_b, H_B, n_pages * PAGE_SIZE), F32),
                   jax.ShapeDtypeStruct((n_b, H_B, LANES), F32)],
        compiler_params=_cparams(("parallel", "arbitrary")),
        name="cumsum_pages",
    )(page_table, *([logf_t] * n_group), logf_new)


def _xattn_kernel(q_ref, k_ref, v_ref, o_ref, *, rows):
    scale = XA_HEAD_DIM ** -0.5
    for bi in range(k_ref.shape[0]):
        rs = slice(bi * rows, (bi + 1) * rows)
        for h in range(XA_HEADS):
            cs = slice(h * XA_HEAD_DIM, (h + 1) * XA_HEAD_DIM)
            q = q_ref[rs, cs].astype(BF16)
            k = k_ref[bi, :, cs].astype(BF16)
            v = v_ref[bi, :, cs].astype(BF16)
            logits = _nt_dot(q, k) * scale
            m = jnp.max(logits, axis=-1, keepdims=True)
            p = jnp.exp(logits - m)
            p = p / jnp.sum(p, axis=-1, keepdims=True)
            o_ref[rs, cs] = jnp.dot(p.astype(BF16), v, preferred_element_type=F32)


def xattn(q, mem_k, mem_v, row0, rows, n_bblk):
    dq = q.shape[1]
    n_b, n_mem, _ = mem_k.shape
    if rows >= 8:
        tq = min(rows, 512)
        per = rows // tq
        steps = n_b * per
        m_spec = pl.BlockSpec((1, n_mem, dq), lambda i: (i // per, 0, 0))
        kern_rows = tq
    else:
        tq = rows * n_bblk
        steps = n_b // n_bblk
        m_spec = pl.BlockSpec((n_bblk, n_mem, dq), lambda i: (i, 0, 0))
        kern_rows = rows
    assert row0 % tq == 0
    return pl.pallas_call(
        functools.partial(_xattn_kernel, rows=kern_rows),
        grid=(steps,),
        in_specs=[pl.BlockSpec((tq, dq), lambda i: (row0 // tq + i, 0)), m_spec, m_spec],
        out_specs=pl.BlockSpec((tq, dq), lambda i: (i, 0)),
        out_shape=jax.ShapeDtypeStruct((n_b * rows, dq), F32),
        compiler_params=_cparams(("parallel",)),
        name="xattn",
    )(q, mem_k, mem_v)


def _router_kernel(x_ref, w_ref, b_ref, idx_ref, gate_ref, rank_ref, cnt_ref, carry_s):
    i = pl.program_id(0)

    @pl.when(i == 0)
    def _():
        carry_s[...] = jnp.zeros_like(carry_s)

    logits = jnp.dot(x_ref[...], w_ref[...], precision=_HI, preferred_element_type=F32) + b_ref[...]
    tm = logits.shape[0]
    lane = lax.broadcasted_iota(I32, logits.shape, 1)
    lane_k = lax.broadcasted_iota(I32, (tm, TOP_K), 1)
    vals = jnp.zeros((tm, TOP_K), F32)
    idxs = jnp.zeros((tm, TOP_K), I32)
    picks = []
    cur = logits
    for r in range(TOP_K):
        m = jnp.max(cur, axis=-1, keepdims=True)
        ix = jnp.min(jnp.where(cur == m, lane, N_EXPERTS), axis=-1, keepdims=True)
        vals = jnp.where(lane_k == r, m, vals)
        idxs = jnp.where(lane_k == r, ix, idxs)
        picks.append(lane == ix)
        cur = jnp.where(picks[-1], -jnp.inf, cur)
    e = jnp.exp(vals - jnp.max(vals, axis=-1, keepdims=True))
    gate_ref[...] = e / jnp.sum(e, axis=-1, keepdims=True)
    idx_ref[...] = idxs

    chosen = jnp.zeros(logits.shape, F32)
    for pk in picks:
        chosen = jnp.where(pk, 1.0, chosen)
    r_i = lax.broadcasted_iota(I32, (tm, tm), 0)
    c_i = lax.broadcasted_iota(I32, (tm, tm), 1)
    before = jnp.where(c_i < r_i, 1.0, 0.0).astype(BF16)
    base = jnp.dot(before, chosen.astype(BF16), preferred_element_type=F32) + carry_s[...]
    ranks = jnp.zeros((tm, TOP_K), F32)
    for r, pk in enumerate(picks):
        ranks = jnp.where(lane_k == r, jnp.sum(jnp.where(pk, base, 0.0), axis=-1, keepdims=True), ranks)
    rank_ref[...] = ranks.astype(I32)
    carry_s[...] += jnp.sum(chosen, axis=0, keepdims=True)
    cnt_ref[...] = carry_s[...].astype(I32)


def router(x, w_router, b_router, tm=ROW_TILE):
    m, d = x.shape
    return pl.pallas_call(
        _router_kernel,
        grid=(m // tm,),
        in_specs=[pl.BlockSpec((tm, d), lambda i: (i, 0)),
                  pl.BlockSpec((d, N_EXPERTS), lambda i: (0, 0)),
                  pl.BlockSpec((1, N_EXPERTS), lambda i: (0, 0))],
        out_specs=[pl.BlockSpec((tm, TOP_K), lambda i: (i, 0)),
                   pl.BlockSpec((tm, TOP_K), lambda i: (i, 0)),
                   pl.BlockSpec((tm, TOP_K), lambda i: (i, 0)),
                   pl.BlockSpec((1, N_EXPERTS), lambda i: (0, 0))],
        out_shape=[jax.ShapeDtypeStruct((m, TOP_K), I32), jax.ShapeDtypeStruct((m, TOP_K), F32),
                   jax.ShapeDtypeStruct((m, TOP_K), I32), jax.ShapeDtypeStruct((1, N_EXPERTS), I32)],
        scratch_shapes=[pltpu.VMEM((1, N_EXPERTS), F32)],
        compiler_params=_cparams(("arbitrary",)),
        name="router",
    )(x, w_router, b_router.reshape(1, N_EXPERTS))


def _expert_kernel(bexp_ref, bval_ref, x_ref, wgu_ref, bgu_ref, wd_ref, bd_ref, o_ref, wgu_s, wd_s):
    i = pl.program_id(0)
    prev = bexp_ref[jnp.maximum(i - 1, 0)]

    @pl.when((i == 0) | (bexp_ref[i] != prev))
    def _():
        wgu_s[...] = wgu_ref[0].astype(BF16)
        wd_s[...] = wd_ref[0].astype(BF16)

    @pl.when(bval_ref[i] > 0)
    def _():
        xb = x_ref[...].astype(BF16)
        de = wd_s.shape[0]
        g = jnp.dot(xb, wgu_s[:, :de], preferred_element_type=F32) + bgu_ref[0, :, :de]
        u = jnp.dot(xb, wgu_s[:, de:], preferred_element_type=F32) + bgu_ref[0, :, de:]
        g = jnp.minimum(g, SWIGLU_LIMIT)
        u = jnp.clip(u, -SWIGLU_LIMIT, SWIGLU_LIMIT)
        act = g * jax.nn.sigmoid(SWIGLU_ALPHA * g) * (u + 1.0)
        o_ref[...] = jnp.dot(act.astype(BF16), wd_s[...], preferred_element_type=F32) + bd_ref[0]

    @pl.when(bval_ref[i] == 0)
    def _():
        o_ref[...] = jnp.zeros_like(o_ref)


def expert_blocks(x_rows, block_exp, block_valid, w_gu, b_gu, w_down, b_down, layer):
    n_rows, d = x_rows.shape
    n_blocks = n_rows // MOE_TILE
    de = w_down.shape[2]
    grid_spec = pltpu.PrefetchScalarGridSpec(
        num_scalar_prefetch=2,
        grid=(n_blocks,),
        in_specs=[pl.BlockSpec((MOE_TILE, d), lambda i, be, bv: (i, 0)),
                  pl.BlockSpec((None, 1, d, 2 * de), lambda i, be, bv: (layer, be[i], 0, 0)),
                  pl.BlockSpec((None, 1, 1, 2 * de), lambda i, be, bv: (layer, be[i], 0, 0)),
                  pl.BlockSpec((None, 1, de, d), lambda i, be, bv: (layer, be[i], 0, 0)),
                  pl.BlockSpec((None, 1, 1, d), lambda i, be, bv: (layer, be[i], 0, 0))],
        out_specs=pl.BlockSpec((MOE_TILE, d), lambda i, be, bv: (i, 0)),
        scratch_shapes=[pltpu.VMEM((d, 2 * de), BF16), pltpu.VMEM((de, d), BF16)],
    )
    return pl.pallas_call(
        _expert_kernel,
        grid_spec=grid_spec,
        out_shape=jax.ShapeDtypeStruct((n_rows, d), F32),
        compiler_params=_cparams(("arbitrary",)),
        name="experts",
    )(block_exp, block_valid, x_rows, w_gu, b_gu.reshape(*b_gu.shape[:2], 1, 2 * de),
      w_down, b_down.reshape(*b_down.shape[:2], 1, d))


def _combine_ln_kernel(y_ref, gate_ref, x_ref, g_ref, b_ref, o_ref):
    y = y_ref[0] * gate_ref[:, 0:1]
    for k in range(1, TOP_K):
        y = y + y_ref[k] * gate_ref[:, k:k + 1]
    o_ref[...] = _layer_norm(DN_ALPHA * x_ref[...] + y, g_ref[...], b_ref[...])


def combine_ln(y_slots, gate, x, g, b, tm=ROW_TILE):
    m, d = x.shape
    return pl.pallas_call(
        _combine_ln_kernel,
        grid=(m // tm,),
        in_specs=[pl.BlockSpec((TOP_K, tm, d), lambda i: (0, i, 0)),
                  pl.BlockSpec((tm, TOP_K), lambda i: (i, 0)),
                  pl.BlockSpec((tm, d), lambda i: (i, 0)),
                  pl.BlockSpec((1, d), lambda i: (0, 0)),
                  pl.BlockSpec((1, d), lambda i: (0, 0))],
        out_specs=pl.BlockSpec((tm, d), lambda i: (i, 0)),
        out_shape=jax.ShapeDtypeStruct((m, d), F32),
        compiler_params=_cparams(("parallel",)),
        name="moe_combine_ln",
    )(y_slots, gate, x, g.reshape(1, d), b.reshape(1, d))


def moe_ln(x, w_router, b_router, w_gu, b_gu, w_down, b_down, layer, g, b):
    n_tok, d = x.shape
    top_idx, gate, rank, counts = router(x, w_router[layer], b_router[layer])
    counts = counts.reshape(N_EXPERTS)
    n_assign = n_tok * TOP_K
    padded = (counts + MOE_TILE - 1) // MOE_TILE * MOE_TILE
    pad_end = jnp.cumsum(padded)
    pad_start = pad_end - padded
    pos = pad_start[top_idx] + rank
    n_blocks = -(-n_assign // MOE_TILE) + N_EXPERTS
    n_rows = n_blocks * MOE_TILE
    tok = jnp.broadcast_to(jnp.arange(n_tok, dtype=I32)[:, None], (n_tok, TOP_K))
    row_tok = jnp.full((n_rows,), n_tok, I32).at[pos.reshape(-1)].set(tok.reshape(-1), unique_indices=True)
    block_start = jnp.arange(n_blocks, dtype=I32) * MOE_TILE
    block_exp = jnp.minimum(jnp.sum((pad_end[None, :] <= block_start[:, None]).astype(I32), axis=1), N_EXPERTS - 1)
    block_valid = (block_start < pad_end[-1]).astype(I32)
    x_pad = jnp.concatenate([x, jnp.zeros((1, d), x.dtype)], axis=0)
    y_rows = expert_blocks(x_pad[row_tok], block_exp, block_valid, w_gu, b_gu, w_down, b_down, layer)
    y_slots = y_rows[pos.T.reshape(-1)].reshape(TOP_K, n_tok, d)
    return combine_ln(y_slots, gate, x, g, b)


def _new_page_t(dst_ref, new_rows):
    dst_ref[...] = jnp.zeros_like(dst_ref)
    dst_ref[0:new_rows.shape[0], :] = new_rows
    return dst_ref[...].T


def _pages(refs):
    return jnp.concatenate([r[...] for r in refs], axis=1)


def _dsa_scores_kernel(pt_ref, *refs, n_group):
    page_refs = refs[:n_group]
    iq_ref, iw_ref, iknew_ref, sc_ref, scnew_ref, new_s = refs[n_group:]
    p = pl.program_id(1)
    iqm = iq_ref[0].astype(BF16)
    iw = iw_ref[0]
    n_q = iqm.shape[0] // IDX_HEADS

    def score_of(ik_t):
        dots = jnp.dot(iqm, ik_t.astype(BF16), preferred_element_type=F32)
        w = jnp.maximum(dots, 0.0) * iw
        return jnp.sum(w.reshape(n_q, IDX_HEADS, ik_t.shape[1]), axis=1)

    sc_ref[0] = score_of(_pages(page_refs))

    @pl.when(p == 0)
    def _():
        n_new = iknew_ref.shape[1]
        s = score_of(_new_page_t(new_s, iknew_ref[0]))
        t = lax.broadcasted_iota(I32, s.shape, 0)
        c = lax.broadcasted_iota(I32, s.shape, 1)
        scnew_ref[0] = jnp.where((c <= t) & (c < n_new), s, -jnp.inf)


def dsa_scores_decode(idxk_t, layer, page_table, iq_rows, iw_rows, ik_new, n_group=32):
    n_b, n_pages = page_table.shape
    n_group = math.gcd(n_group, n_pages)
    n_q = iq_rows.shape[1] // IDX_HEADS
    steps = n_pages // n_group

    def page_map(g):
        return lambda b, p, pt: (layer, pt[b, p * n_group + g], 0, 0)

    grid_spec = pltpu.PrefetchScalarGridSpec(
        num_scalar_prefetch=1,
        grid=(n_b, steps),
        in_specs=[pl.BlockSpec((None, None, IDX_DIM, PAGE_SIZE), page_map(g)) for g in range(n_group)]
        + [pl.BlockSpec((1,) + iq_rows.shape[1:], lambda b, p, pt: (b, 0, 0)),
           pl.BlockSpec((1,) + iw_rows.shape[1:], lambda b, p, pt: (b, 0, 0)),
           pl.BlockSpec((1,) + ik_new.shape[1:], lambda b, p, pt: (b, 0, 0))],
        out_specs=[pl.BlockSpec((1, n_q, n_group * PAGE_SIZE), lambda b, p, pt: (b, 0, p)),
                   pl.BlockSpec((1, n_q, PAGE_SIZE), lambda b, p, pt: (b, 0, 0))],
        scratch_shapes=[pltpu.VMEM((PAGE_SIZE, IDX_DIM), F32)],
    )
    return pl.pallas_call(
        functools.partial(_dsa_scores_kernel, n_group=n_group),
        grid_spec=grid_spec,
        out_shape=[jax.ShapeDtypeStruct((n_b, n_q, n_pages * PAGE_SIZE), F32),
                   jax.ShapeDtypeStruct((n_b, n_q, PAGE_SIZE), F32)],
        compiler_params=_cparams(("parallel", "arbitrary")),
        name="dsa_scores_decode",
    )(page_table, *([idxk_t] * n_group), iq_rows, iw_rows, ik_new)


def _select_kernel(s_ref, sel_ref, cut_ref, *, n_sel):
    key = _sort_key(s_ref[...])
    thr = _topk_threshold(key, n_sel, cut_ref)
    idx = lax.broadcasted_iota(I32, key.shape, 1)
    sel = (key > thr) | ((key == thr) & (idx <= cut_ref[...]))
    sel_ref[...] = jnp.where(sel, 1.0, 0.0)


def topk_select(scores, n_sel):
    return pl.pallas_call(
        functools.partial(_select_kernel, n_sel=n_sel),
        out_shape=jax.ShapeDtypeStruct(scores.shape, F32),
        scratch_shapes=[pltpu.VMEM((scores.shape[0], 1), I32)],
        compiler_params=pltpu.CompilerParams(vmem_limit_bytes=VMEM_LIMIT),
        name="topk_select",
    )(scores)


def _softmax_update(logits, mask, vt_bf16, m_s, l_s, acc_s):
    if mask is not None:
        logits = jnp.where(mask, logits, NEG_INF)
    m_old = m_s[...]
    m_new = jnp.maximum(m_old, jnp.max(logits, axis=-1, keepdims=True))
    alpha = jnp.exp(m_old - m_new)
    pe = jnp.exp(logits - m_new)
    if mask is not None:
        pe = jnp.where(mask, pe, 0.0)
    l_s[...] = alpha * l_s[...] + jnp.sum(pe, axis=-1, keepdims=True)
    acc_s[...] = alpha * acc_s[...] + _nt_dot(pe.astype(BF16), vt_bf16)
    m_s[...] = m_new


def _expand_rows(x, reps):
    n_q, s = x.shape
    return jnp.broadcast_to(x[:, None, :], (n_q, reps, s)).reshape(n_q * reps, s)


def _softmax_init(m_s, l_s, acc_s):
    m_s[...] = jnp.full_like(m_s, NEG_INF)
    l_s[...] = jnp.zeros_like(l_s)
    acc_s[...] = jnp.zeros_like(acc_s)


def _dsa_attn_decode_kernel(pt_ref, *refs, n_group):
    k_refs = refs[:n_group]
    v_refs = refs[n_group:2 * n_group]
    (qbd_ref, kvnew_ref, sel_ref, selnew_ref, bias_ref, biasnew_ref, o_ref,
     m_s, l_s, acc_s, knew_s, vnew_s) = refs[2 * n_group:]
    p = pl.program_id(1)
    kw = KV_A * HEAD_DIM

    @pl.when(p == 0)
    def _():
        _softmax_init(m_s, l_s, acc_s)

    qbd = (qbd_ref[0] * ATTN_SCALE).astype(BF16)
    logits = jnp.dot(qbd, _pages(k_refs).astype(BF16), preferred_element_type=F32) + bias_ref[...]
    mask = _expand_rows(sel_ref[0], H_A) > 0.5
    _softmax_update(logits, mask, _pages(v_refs).astype(BF16), m_s, l_s, acc_s)

    @pl.when(p == pl.num_programs(1) - 1)
    def _():
        n_new = kvnew_ref.shape[1]
        kt = _new_page_t(knew_s, kvnew_ref[0, :, :kw]).astype(BF16)
        vt = _new_page_t(vnew_s, kvnew_ref[0, :, kw:]).astype(BF16)
        lg = jnp.dot(qbd, kt, preferred_element_type=F32) + biasnew_ref[...]
        t = lax.broadcasted_iota(I32, lg.shape, 0) // H_A
        c = lax.broadcasted_iota(I32, lg.shape, 1)
        mk = (_expand_rows(selnew_ref[0], H_A) > 0.5) & (c <= t) & (c < n_new)
        _softmax_update(lg, mk, vt, m_s, l_s, acc_s)
        o_ref[0] = acc_s[...] / l_s[...]


def _decode_specs(n_group, width, page_map, rows, extra):
    page_spec = [pl.BlockSpec((None, None, width, PAGE_SIZE), page_map(g)) for g in range(n_group)]
    return page_spec + page_spec + [pl.BlockSpec((1, rows, width), lambda b, p, pt: (b, 0, 0))] + extra


def dsa_attn_decode(k_t, v_t, layer, page_table, qbd, kv_new, sel, bias, n_group=16):
    n_b, n_pages = page_table.shape
    n_group = math.gcd(n_group, n_pages)
    rows = qbd.shape[1]
    n_q = rows // H_A
    kw = KV_A * HEAD_DIM
    steps = n_pages // n_group
    gw = n_group * PAGE_SIZE

    def page_map(g):
        return lambda b, p, pt: (layer, pt[b, p * n_group + g], 0, 0)

    grid_spec = pltpu.PrefetchScalarGridSpec(
        num_scalar_prefetch=1,
        grid=(n_b, steps),
        in_specs=_decode_specs(n_group, kw, page_map, rows, [
            pl.BlockSpec((1,) + kv_new.shape[1:], lambda b, p, pt: (b, 0, 0)),
            pl.BlockSpec((1, n_q, gw), lambda b, p, pt: (b, 0, p)),
            pl.BlockSpec((1, n_q, PAGE_SIZE), lambda b, p, pt: (b, 0, n_pages)),
            pl.BlockSpec((rows, gw), lambda b, p, pt: (0, p)),
            pl.BlockSpec((rows, PAGE_SIZE), lambda b, p, pt: (0, n_pages))]),
        out_specs=pl.BlockSpec((1, rows, kw), lambda b, p, pt: (b, 0, 0)),
        scratch_shapes=[pltpu.VMEM((rows, 1), F32), pltpu.VMEM((rows, 1), F32), pltpu.VMEM((rows, kw), F32),
                        pltpu.VMEM((PAGE_SIZE, kw), F32), pltpu.VMEM((PAGE_SIZE, kw), F32)],
    )
    return pl.pallas_call(
        functools.partial(_dsa_attn_decode_kernel, n_group=n_group),
        grid_spec=grid_spec,
        out_shape=jax.ShapeDtypeStruct((n_b, rows, kw), F32),
        compiler_params=_cparams(("parallel", "arbitrary")),
        name="dsa_attn_decode",
    )(page_table, *([k_t] * n_group), *([v_t] * n_group), qbd, kv_new, sel, sel, bias, bias)


def _fox_attn_decode_kernel(pt_ref, *refs, n_group):
    k_refs = refs[:n_group]
    v_refs = refs[n_group:2 * n_group]
    (qbd_ref, knew_ref, vnew_ref, cq_ref, ck_ref, cknew_ref, o_ref,
     m_s, l_s, acc_s, knew_s, vnew_s) = refs[2 * n_group:]
    p = pl.program_id(1)
    n_q = qbd_ref.shape[1] // H_B

    @pl.when(p == 0)
    def _():
        _softmax_init(m_s, l_s, acc_s)

    qbd = (qbd_ref[0] * ATTN_SCALE).astype(BF16)
    cq = cq_ref[0]
    ck = jnp.concatenate([ck_ref[0]] * n_q, axis=0)
    logits = jnp.dot(qbd, _pages(k_refs).astype(BF16), preferred_element_type=F32) + cq - ck
    _softmax_update(logits, None, _pages(v_refs).astype(BF16), m_s, l_s, acc_s)

    @pl.when(p == pl.num_programs(1) - 1)
    def _():
        n_new = knew_ref.shape[1]
        kt = _new_page_t(knew_s, knew_ref[0]).astype(BF16)
        vt = _new_page_t(vnew_s, vnew_ref[0]).astype(BF16)
        ckn = jnp.concatenate([cknew_ref[0]] * n_q, axis=0)
        lg = jnp.dot(qbd, kt, preferred_element_type=F32) + cq - ckn
        t = lax.broadcasted_iota(I32, lg.shape, 0) // H_B
        c = lax.broadcasted_iota(I32, lg.shape, 1)
        _softmax_update(lg, (c <= t) & (c < n_new), vt, m_s, l_s, acc_s)
        o_ref[0] = acc_s[...] / l_s[...]


def fox_attn_decode(k_t, v_t, layer, page_table, qbd, k_new, v_new, cq_rows, ck, ck_new, n_group=8):
    n_b, n_pages = page_table.shape
    n_group = math.gcd(n_group, n_pages)
    rows = qbd.shape[1]
    kw = H_B * HEAD_DIM
    steps = n_pages // n_group
    gw = n_group * PAGE_SIZE

    def page_map(g):
        return lambda b, p, pt: (layer, pt[b, p * n_group + g], 0, 0)

    grid_spec = pltpu.PrefetchScalarGridSpec(
        num_scalar_prefetch=1,
        grid=(n_b, steps),
        in_specs=_decode_specs(n_group, kw, page_map, rows, [
            pl.BlockSpec((1,) + k_new.shape[1:], lambda b, p, pt: (b, 0, 0)),
            pl.BlockSpec((1,) + v_new.shape[1:], lambda b, p, pt: (b, 0, 0)),
            pl.BlockSpec((1, rows, 1), lambda b, p, pt: (b, 0, 0)),
            pl.BlockSpec((1, H_B, gw), lambda b, p, pt: (b, 0, p)),
            pl.BlockSpec((1, H_B, PAGE_SIZE), lambda b, p, pt: (b, 0, 0))]),
        out_specs=pl.BlockSpec((1, rows, kw), lambda b, p, pt: (b, 0, 0)),
        scratch_shapes=[pltpu.VMEM((rows, 1), F32), pltpu.VMEM((rows, 1), F32), pltpu.VMEM((rows, kw), F32),
                        pltpu.VMEM((PAGE_SIZE, kw), F32), pltpu.VMEM((PAGE_SIZE, kw), F32)],
    )
    return pl.pallas_call(
        functools.partial(_fox_attn_decode_kernel, n_group=n_group),
        grid_spec=grid_spec,
        out_shape=jax.ShapeDtypeStruct((n_b, rows, kw), F32),
        compiler_params=_cparams(("parallel", "arbitrary")),
        name="fox_attn_decode",
    )(page_table, *([k_t] * n_group), *([v_t] * n_group), qbd, k_new, v_new, cq_rows, ck, ck_new)


def _sb_attn_decode_kernel(pt_ref, *refs, n_group):
    k_refs = refs[:n_group]
    v_refs = refs[n_group:2 * n_group]
    qbd_ref, knew_ref, vnew_ref, o_ref, carry_s, acc_s, knew_s, vnew_s = refs[2 * n_group:]
    p = pl.program_id(1)
    qbd = (qbd_ref[0] * ATTN_SCALE).astype(BF16)
    tri2 = _strict_lower_twice(LANES)

    def chunk(kt_bf16, vt_bf16, mask):
        z = jnp.dot(qbd, kt_bf16, preferred_element_type=F32)
        w_keys = z.shape[1]
        n_sub = w_keys // LANES
        log_keep = -(jnp.maximum(z, 0.0) + jnp.log(1.0 + jnp.exp(-jnp.abs(z))))
        log_hit = z + log_keep
        if mask is not None:
            log_keep = jnp.where(mask, log_keep, 0.0)
        subs = [log_keep[:, s * LANES:(s + 1) * LANES] for s in range(n_sub)]
        sums = [jnp.sum(x, axis=-1, keepdims=True) for x in subs]
        afters = [None] * n_sub
        right = carry_s[...]
        for s in range(n_sub - 1, -1, -1):
            afters[s] = _split_suffix(subs[s], tri2) + right
            right = right + sums[s]
        w = jnp.exp(log_hit + jnp.concatenate(afters, axis=1))
        if mask is not None:
            w = jnp.where(mask, w, 0.0)
        acc_s[...] += _nt_dot(w.astype(BF16), vt_bf16)
        carry_s[...] = right

    @pl.when(p == 0)
    def _():
        carry_s[...] = jnp.zeros_like(carry_s)
        acc_s[...] = jnp.zeros_like(acc_s)
        n_new = knew_ref.shape[1]
        kt = _new_page_t(knew_s, knew_ref[0]).astype(BF16)
        vt = _new_page_t(vnew_s, vnew_ref[0]).astype(BF16)
        t = lax.broadcasted_iota(I32, (qbd.shape[0], PAGE_SIZE), 0) // H_C
        c = lax.broadcasted_iota(I32, (qbd.shape[0], PAGE_SIZE), 1)
        chunk(kt, vt, (c < t) & (c < n_new))

    chunk(_pages(k_refs).astype(BF16), _pages(v_refs).astype(BF16), None)

    @pl.when(p == pl.num_programs(1) - 1)
    def _():
        o_ref[0] = acc_s[...]


def sb_attn_decode(k_t, v_t, layer, page_table, qbd, k_new, v_new, n_group=8):
    n_b, n_pages = page_table.shape
    n_group = math.gcd(n_group, n_pages)
    rows = qbd.shape[1]
    kw = H_C * HEAD_DIM
    steps = n_pages // n_group

    def page_map(g):
        return lambda b, p, pt: (layer, pt[b, (steps - 1 - p) * n_group + g], 0, 0)

    grid_spec = pltpu.PrefetchScalarGridSpec(
        num_scalar_prefetch=1,
        grid=(n_b, steps),
        in_specs=_decode_specs(n_group, kw, page_map, rows, [
            pl.BlockSpec((1,) + k_new.shape[1:], lambda b, p, pt: (b, 0, 0)),
            pl.BlockSpec((1,) + v_new.shape[1:], lambda b, p, pt: (b, 0, 0))]),
        out_specs=pl.BlockSpec((1, rows, kw), lambda b, p, pt: (b, 0, 0)),
        scratch_shapes=[pltpu.VMEM((rows, 1), F32), pltpu.VMEM((rows, kw), F32),
                        pltpu.VMEM((PAGE_SIZE, kw), F32), pltpu.VMEM((PAGE_SIZE, kw), F32)],
    )
    return pl.pallas_call(
        functools.partial(_sb_attn_decode_kernel, n_group=n_group),
        grid_spec=grid_spec,
        out_shape=jax.ShapeDtypeStruct((n_b, rows, kw), F32),
        compiler_params=_cparams(("parallel", "arbitrary")),
        name="sb_attn_decode",
    )(page_table, *([k_t] * n_group), *([v_t] * n_group), qbd, k_new, v_new)


def _block_diag_queries(q, group):
    n_b, n_q, heads, hd = q.shape
    onehot = jax.nn.one_hot(jnp.arange(heads) // group, heads // group, dtype=q.dtype)
    out = q[:, :, :, None, :] * onehot[None, None, :, :, None]
    return out.reshape(n_b, n_q * heads, (heads // group) * hd)


def _take_diag(o, n_q, heads, group):
    n_b = o.shape[0]
    groups = heads // group
    onehot = jax.nn.one_hot(jnp.arange(heads) // group, groups, dtype=o.dtype)
    o = o.reshape(n_b, n_q, heads, groups, HEAD_DIM)
    return jnp.sum(o * onehot[None, None, :, :, None], axis=3).reshape(n_b * n_q, heads * HEAD_DIM)


def _ab_weight(w):
    sizes = (H_A * HEAD_DIM, KV_A * HEAD_DIM, KV_A * HEAD_DIM, IDX_HEADS * IDX_DIM, IDX_DIM, IDX_HEADS,
             H_B * HEAD_DIM, H_B * HEAD_DIM, H_B * HEAD_DIM, H_B)
    offs = [0]
    for sz in sizes:
        offs.append(offs[-1] + sz)
    part = [w[:, offs[i]:offs[i + 1]] for i in range(len(sizes))]
    qa, ka, va, iq, ik, iw, qb, kb, vb, fl = part
    pad = jnp.zeros((w.shape[0], LANES - IDX_DIM - IDX_HEADS - H_B), w.dtype)
    return jnp.concatenate([qa, ka, va, iq, ik, iw, fl, pad, qb, kb, vb], axis=1).astype(BF16)


def kernel(x_prompt, x_sample, cache_a_k, cache_a_v, cache_a_idxk, cache_b_k, cache_b_v, cache_b_logf, cache_c_k, cache_c_v, cache_mem_k, cache_mem_v, page_table, mem_prompt, w_in_ab, b_forget, w_out_ab, w_in_c, w_out_c, rel_bias, w_xq, w_xkv, w_xo, ln_g, ln_b, w_router, b_router, w_gate_up, b_gate_up, w_down, b_down):
    n_b, t, d = x_prompt.shape
    s_b, s_t, _ = x_sample.shape
    n_p = n_b * t
    n_s = s_b * s_t
    n_mem = mem_prompt.shape[1]
    past = page_table.shape[1] * PAGE_SIZE
    kvw = KV_A * HEAD_DIM
    xw = XA_HEADS * XA_HEAD_DIM

    x = jnp.concatenate([x_prompt.reshape(n_p, d), x_sample.reshape(n_s, d)], axis=0)
    win = bias_window(rel_bias)
    far = rel_bias[N_BUCKETS - 1]
    brow = bias_rows(rel_bias, past, s_t, past + PAGE_SIZE)
    mem_p = mem_prompt.reshape(n_b * n_mem, d)

    def pages_t(c):
        ct = jnp.moveaxis(c, 2, -1)
        return ct.reshape(ct.shape[0], ct.shape[1], -1, PAGE_SIZE)

    ak_t, av_t, ik_t = pages_t(cache_a_k), pages_t(cache_a_v), pages_t(cache_a_idxk)
    bk_t, bv_t, lf_t = pages_t(cache_b_k), pages_t(cache_b_v), pages_t(cache_b_logf)
    ck_t, cv_t = pages_t(cache_c_k), pages_t(cache_c_v)

    rows_ab_p, rows_ab_s, rows_c_p, rows_c_s, mem_kv = [], [], [], [], []
    for li in range(DEPTH):
        j = li // 2
        if li % 2 == 0:
            bf_row = jnp.zeros((1, LANES), F32).at[0, MISC_FL:MISC_FL + H_B].set(b_forget[j])
            qa, kva, iq, misc, qb, kb, vb = ab_proj(x, _ab_weight(w_in_ab[j]), bf_row)
            o_a_p = dsa_prompt(qa, iq, misc, kva, win, far, n_b, t)
            cum, cumt = cumsum_prompt(misc, n_b, t)
            o_b_p = fox_prompt(qb, kb, vb, cum, cumt, n_b, t)
            misc_s = misc[n_p:].reshape(s_b, s_t, LANES)
            ik_new = misc_s[:, :, MISC_IK:MISC_IK + IDX_DIM]
            iw_rows = misc_s[:, :, MISC_IW:MISC_IW + IDX_HEADS].reshape(s_b, s_t * IDX_HEADS, 1)
            logf_new = misc_s[:, :, MISC_FL:MISC_FL + H_B]
            iq_rows = iq[n_p:].reshape(s_b, s_t * IDX_HEADS, IDX_DIM)
            kva_s = kva[n_p:].reshape(s_b, s_t, 2 * kvw)
            sc_past, sc_new = dsa_scores_decode(ik_t, j, page_table, iq_rows, iw_rows, ik_new)
            scores = jnp.concatenate([sc_past, sc_new], axis=-1).reshape(n_s, past + PAGE_SIZE)
            sel = topk_select(scores, min(TOPK_MAX, (past + s_t) // 4)).reshape(s_b, s_t, past + PAGE_SIZE)
            qbd_a = _block_diag_queries(qa[n_p:].reshape(s_b, s_t, H_A, HEAD_DIM), H_A // KV_A)
            o_a_s = _take_diag(dsa_attn_decode(ak_t, av_t, j, page_table, qbd_a, kva_s, sel, brow),
                               s_t, H_A, H_A // KV_A)
            ck, ck_new = cumsum_pages(lf_t, j, page_table, logf_new)
            cq_rows = jnp.swapaxes(ck_new[:, :, :s_t], 1, 2).reshape(s_b, s_t * H_B, 1)
            qbd_b = _block_diag_queries(qb[n_p:].reshape(s_b, s_t, H_B, HEAD_DIM), 1)
            kb_s = kb[n_p:].reshape(s_b, s_t, H_B * HEAD_DIM)
            vb_s = vb[n_p:].reshape(s_b, s_t, H_B * HEAD_DIM)
            o_b_s = _take_diag(fox_attn_decode(bk_t, bv_t, j, page_table, qbd_b, kb_s, vb_s,
                                               cq_rows, ck, ck_new), s_t, H_B, 1)
            o_a = jnp.concatenate([o_a_p, o_a_s], axis=0)
            o_b = jnp.concatenate([o_b_p, o_b_s], axis=0)
            w_out = w_out_ab[j].astype(BF16)
            x = linear2_res_ln(o_a, w_out[:H_A * HEAD_DIM], o_b, w_out[H_A * HEAD_DIM:], x,
                               ln_g[li, 0], ln_b[li, 0])
            rows = (kva[:, :kvw], kva[:, kvw:], misc[:, MISC_IK:MISC_IK + IDX_DIM], kb, vb,
                    misc[:, MISC_FL:MISC_FL + H_B])
            tails = ((KV_A, HEAD_DIM), (KV_A, HEAD_DIM), (IDX_DIM,), (H_B, HEAD_DIM), (H_B, HEAD_DIM), (H_B,))
            rows_ab_p.append(tuple(r[:n_p].reshape(n_b, t, *tl) for r, tl in zip(rows, tails)))
            rows_ab_s.append(tuple(r[n_p:].reshape(s_b, s_t, *tl) for r, tl in zip(rows, tails)))
        else:
            q, k, v = linear(x, w_in_c[j].astype(BF16), (H_C * HEAD_DIM,) * 3, name="c_proj")
            o_p = sb_prompt(q, k, v, n_b, t)
            qbd = _block_diag_queries(q[n_p:].reshape(s_b, s_t, H_C, HEAD_DIM), 1)
            k_s = k[n_p:].reshape(s_b, s_t, H_C * HEAD_DIM)
            v_s = v[n_p:].reshape(s_b, s_t, H_C * HEAD_DIM)
            o_s = _take_diag(sb_attn_decode(ck_t, cv_t, j, page_table, qbd, k_s, v_s), s_t, H_C, 1)
            o = jnp.concatenate([o_p, o_s], axis=0)
            x = linear_res_ln(o, w_out_c[j].astype(BF16), x, ln_g[li, 0], ln_b[li, 0], name="c_out")
            rows_c_p.append((k[:n_p].reshape(n_b, t, H_C, HEAD_DIM), v[:n_p].reshape(n_b, t, H_C, HEAD_DIM)))
            rows_c_s.append((k_s.reshape(s_b, s_t, H_C, HEAD_DIM), v_s.reshape(s_b, s_t, H_C, HEAD_DIM)))

        mk_p, mv_p = linear(mem_p, w_xkv[li].astype(BF16), (xw, xw), tm=256, name="mem_proj")
        mem_kv.append((mk_p, mv_p))
        (xq,) = linear(x, w_xq[li].astype(BF16), (xw,), name="xq_proj")
        o = jnp.concatenate([
            xattn(xq, mk_p.reshape(n_b, n_mem, xw), mv_p.reshape(n_b, n_mem, xw), 0, t, 1),
            xattn(xq, cache_mem_k[li].reshape(s_b, n_mem, xw), cache_mem_v[li].reshape(s_b, n_mem, xw),
                  n_p, s_t, 8)], axis=0)
        x = linear_res_ln(o, w_xo[li].astype(BF16), x, ln_g[li, 1], ln_b[li, 1], name="xo_proj")

        x = moe_ln(x, w_router, b_router, w_gate_up, b_gate_up, w_down, b_down, li, ln_g[li, 2], ln_b[li, 2])

    def stack(rows):
        return tuple(jnp.stack([r[i] for r in rows]) for i in range(len(rows[0])))

    p_mem_k = jnp.stack([kv[0].reshape(n_b, n_mem, XA_HEADS, XA_HEAD_DIM) for kv in mem_kv])
    p_mem_v = jnp.stack([kv[1].reshape(n_b, n_mem, XA_HEADS, XA_HEAD_DIM) for kv in mem_kv])
    return ((x[:n_p].reshape(n_b, t, d), x[n_p:].reshape(s_b, s_t, d))
            + stack(rows_ab_p) + stack(rows_c_p) + (p_mem_k, p_mem_v) + stack(rows_ab_s) + stack(rows_c_s))
```

```python
import functools
import math

import jax
import jax.numpy as jnp
from jax import lax
from jax.experimental import pallas as pl
from jax.experimental.pallas import tpu as pltpu

F32 = jnp.float32
BF16 = jnp.bfloat16
I32 = jnp.int32

D_MODEL = 1024
DEPTH = 4
PAGE_SIZE = 128
HEAD_DIM = 64
H_A = 8
KV_A = 2
IDX_HEADS = 8
IDX_DIM = 64
TOPK_MAX = 256
H_B = 8
H_C = 16
N_BUCKETS = 32
MAX_DISTANCE = 128
XA_HEADS = 4
XA_HEAD_DIM = 128
N_EXPERTS = 32
TOP_K = 4
D_EXPERT = D_MODEL
SWIGLU_LIMIT = 7.0
SWIGLU_ALPHA = 1.702
LN_EPS = 1e-5
NEG_INF = -1e30
ATTN_SCALE = HEAD_DIM ** -0.5
DN_ALPHA = (2 * DEPTH) ** 0.25

LANES = 128
Q_TILE = 128
ROW_TILE = 384
MOE_TILE = 256
VMEM_LIMIT = 56 * 1024 * 1024

MISC_IK = 0
MISC_IW = IDX_DIM
MISC_FL = IDX_DIM + IDX_HEADS

_HI = lax.Precision.HIGHEST


def _cparams(sem):
    return pltpu.CompilerParams(dimension_semantics=sem, vmem_limit_bytes=VMEM_LIMIT)


def _log_sigmoid(x):
    return -(jnp.maximum(-x, 0.0) + jnp.log1p(jnp.exp(-jnp.abs(x))))


def _nt_dot(a, b):
    return lax.dot_general(a, b, (((1,), (1,)), ((), ())), preferred_element_type=F32)


def _linear_kernel(x_ref, w_ref, *o_refs):
    xb = x_ref[...].astype(BF16)
    off = 0
    for o in o_refs:
        n = o.shape[-1]
        o[...] = jnp.dot(xb, w_ref[:, off:off + n], preferred_element_type=F32)
        off += n


def linear(x, w_bf16, splits, tm=ROW_TILE, name="linear"):
    m, k = x.shape
    n = w_bf16.shape[1]
    assert sum(splits) == n and all(s % LANES == 0 for s in splits) and m % tm == 0
    return pl.pallas_call(
        _linear_kernel,
        grid=(m // tm,),
        in_specs=[pl.BlockSpec((tm, k), lambda i: (i, 0)),
                  pl.BlockSpec((k, n), lambda i: (0, 0))],
        out_specs=[pl.BlockSpec((tm, s), lambda i: (i, 0)) for s in splits],
        out_shape=[jax.ShapeDtypeStruct((m, s), F32) for s in splits],
        compiler_params=_cparams(("parallel",)),
        name=name,
    )(x, w_bf16)


def _ab_proj_kernel(x_ref, w_ref, bf_ref, qa_ref, kva_ref, iq_ref, misc_ref, qb_ref, kb_ref, vb_ref):
    xb = x_ref[...].astype(BF16)
    off = 0
    for o in (qa_ref, kva_ref, iq_ref, misc_ref, qb_ref, kb_ref, vb_ref):
        n = o.shape[-1]
        r = jnp.dot(xb, w_ref[:, off:off + n], preferred_element_type=F32)
        if o is misc_ref:
            lane = lax.broadcasted_iota(I32, r.shape, 1)
            is_fl = (lane >= MISC_FL) & (lane < MISC_FL + H_B)
            r = jnp.where(is_fl, _log_sigmoid(r + bf_ref[...]), r)
        o[...] = r
        off += n


def ab_proj(x, w_bf16, bf_row, tm=ROW_TILE):
    m, k = x.shape
    splits = (H_A * HEAD_DIM, 2 * KV_A * HEAD_DIM, IDX_HEADS * IDX_DIM, LANES,
              H_B * HEAD_DIM, H_B * HEAD_DIM, H_B * HEAD_DIM)
    n = sum(splits)
    return pl.pallas_call(
        _ab_proj_kernel,
        grid=(m // tm,),
        in_specs=[pl.BlockSpec((tm, k), lambda i: (i, 0)),
                  pl.BlockSpec((k, n), lambda i: (0, 0)),
                  pl.BlockSpec((1, LANES), lambda i: (0, 0))],
        out_specs=[pl.BlockSpec((tm, s), lambda i: (i, 0)) for s in splits],
        out_shape=[jax.ShapeDtypeStruct((m, s), F32) for s in splits],
        compiler_params=_cparams(("parallel",)),
        name="ab_proj",
    )(x, w_bf16, bf_row)


def _layer_norm(z, g, b):
    mu = jnp.mean(z, axis=-1, keepdims=True)
    zc = z - mu
    var = jnp.mean(zc * zc, axis=-1, keepdims=True)
    return zc * lax.rsqrt(var + LN_EPS) * g + b


def _linear_res_ln_kernel(h_ref, w_ref, x_ref, g_ref, b_ref, o_ref):
    y = jnp.dot(h_ref[...].astype(BF16), w_ref[...], preferred_element_type=F32)
    o_ref[...] = _layer_norm(DN_ALPHA * x_ref[...] + y, g_ref[...], b_ref[...])


def linear_res_ln(h, w_bf16, x, g, b, tm=ROW_TILE, name="linear_res_ln"):
    m, k = h.shape
    d = w_bf16.shape[1]
    return pl.pallas_call(
        _linear_res_ln_kernel,
        grid=(m // tm,),
        in_specs=[pl.BlockSpec((tm, k), lambda i: (i, 0)),
                  pl.BlockSpec((k, d), lambda i: (0, 0)),
                  pl.BlockSpec((tm, d), lambda i: (i, 0)),
                  pl.BlockSpec((1, d), lambda i: (0, 0)),
                  pl.BlockSpec((1, d), lambda i: (0, 0))],
        out_specs=pl.BlockSpec((tm, d), lambda i: (i, 0)),
        out_shape=jax.ShapeDtypeStruct((m, d), F32),
        compiler_params=_cparams(("parallel",)),
        name=name,
    )(h, w_bf16, x, g.reshape(1, d), b.reshape(1, d))


def _linear2_res_ln_kernel(h1_ref, w1_ref, h2_ref, w2_ref, x_ref, g_ref, b_ref, o_ref):
    y = jnp.dot(h1_ref[...].astype(BF16), w1_ref[...], preferred_element_type=F32)
    y = y + jnp.dot(h2_ref[...].astype(BF16), w2_ref[...], preferred_element_type=F32)
    o_ref[...] = _layer_norm(DN_ALPHA * x_ref[...] + y, g_ref[...], b_ref[...])


def linear2_res_ln(h1, w1, h2, w2, x, g, b, tm=ROW_TILE):
    m, k1 = h1.shape
    k2 = h2.shape[1]
    d = w1.shape[1]
    return pl.pallas_call(
        _linear2_res_ln_kernel,
        grid=(m // tm,),
        in_specs=[pl.BlockSpec((tm, k1), lambda i: (i, 0)),
                  pl.BlockSpec((k1, d), lambda i: (0, 0)),
                  pl.BlockSpec((tm, k2), lambda i: (i, 0)),
                  pl.BlockSpec((k2, d), lambda i: (0, 0)),
                  pl.BlockSpec((tm, d), lambda i: (i, 0)),
                  pl.BlockSpec((1, d), lambda i: (0, 0)),
                  pl.BlockSpec((1, d), lambda i: (0, 0))],
        out_specs=pl.BlockSpec((tm, d), lambda i: (i, 0)),
        out_shape=jax.ShapeDtypeStruct((m, d), F32),
        compiler_params=_cparams(("parallel",)),
        name="ab_out",
    )(h1, w1, h2, w2, x, g.reshape(1, d), b.reshape(1, d))


def _t5_bucket(rel):
    n = jnp.maximum(rel, 0)
    max_exact = N_BUCKETS // 2
    nf = jnp.maximum(n, 1).astype(F32)
    large = max_exact + (jnp.log(nf / max_exact) / math.log(MAX_DISTANCE / max_exact)
                         * (N_BUCKETS - max_exact)).astype(I32)
    large = jnp.minimum(large, N_BUCKETS - 1)
    return jnp.where(n < max_exact, n, large)


def _bias_lookup(bucket, tab_col):
    out = jnp.zeros(bucket.shape, F32)
    for b in range(N_BUCKETS):
        out = jnp.where(bucket == b, tab_col[:, b:b + 1], out)
    return out


def _bias_window_kernel(tab_ref, o_ref):
    r = lax.broadcasted_iota(I32, (Q_TILE, 2 * Q_TILE), 0)
    c = lax.broadcasted_iota(I32, (Q_TILE, 2 * Q_TILE), 1)
    bucket = _t5_bucket(r + Q_TILE - c)
    tab_col = jnp.broadcast_to(tab_ref[0], (Q_TILE, N_BUCKETS))
    o_ref[0] = _bias_lookup(bucket, tab_col)


def bias_window(rel_bias):
    tab_t = rel_bias.T.reshape(H_A, 1, N_BUCKETS)
    return pl.pallas_call(
        _bias_window_kernel,
        grid=(H_A,),
        in_specs=[pl.BlockSpec((1, 1, N_BUCKETS), lambda h: (h, 0, 0))],
        out_specs=pl.BlockSpec((1, Q_TILE, 2 * Q_TILE), lambda h: (h, 0, 0)),
        out_shape=jax.ShapeDtypeStruct((H_A, Q_TILE, 2 * Q_TILE), F32),
        compiler_params=_cparams(("parallel",)),
        name="bias_window",
    )(tab_t)


def _bias_rows_kernel(tab_ref, o_ref, *, past, n_q):
    rows, cols = o_ref.shape
    t = lax.broadcasted_iota(I32, (rows, cols), 0) // H_A
    s = lax.broadcasted_iota(I32, (rows, cols), 1)
    o_ref[...] = _bias_lookup(_t5_bucket(past + t - s), tab_ref[...])


def bias_rows(rel_bias, past, n_q, n_cols):
    tab_rows = jnp.tile(rel_bias.T, (n_q, 1))
    return pl.pallas_call(
        functools.partial(_bias_rows_kernel, past=past, n_q=n_q),
        out_shape=jax.ShapeDtypeStruct((n_q * H_A, n_cols), F32),
        compiler_params=pltpu.CompilerParams(vmem_limit_bytes=VMEM_LIMIT),
        name="bias_rows",
    )(tab_rows)


def _sort_key(score):
    bits = pltpu.bitcast(score + 0.0, I32)
    return bits ^ ((bits >> 31) & 0x7FFFFFFF)


def _lane_total(acc):
    return jnp.dot(acc.astype(BF16), jnp.ones((LANES, LANES), BF16), preferred_element_type=F32)


def _count_tiles(pred, n_tiles, rows):
    acc = jnp.zeros((rows, LANES), F32)
    for j in range(n_tiles):
        acc = acc + jnp.where(pred(j), 1.0, 0.0)
    return _lane_total(acc)


def _topk_threshold(key, k, cut_ref):
    r, n = key.shape
    n_tiles = n // LANES
    assert n % LANES == 0 and n_tiles <= 256 and r % 16 == 0
    kf = float(k)
    hr = r // 2
    halves = [[key[h * hr:(h + 1) * hr, j * LANES:(j + 1) * LANES] for j in range(n_tiles)] for h in range(2)]

    def body(it, ans):
        bit = jnp.left_shift(jnp.int32(1), 31 - it)
        out = []
        for h in range(2):
            cand = ans[h] + bit
            cnt = _count_tiles(lambda j: halves[h][j] >= cand, n_tiles, hr)
            out.append(jnp.where(cnt >= kf, cand, ans[h]))
        return tuple(out)

    init = jnp.full((hr, LANES), -2 ** 31, I32)
    thr_rep = lax.fori_loop(0, 32, body, (init, init))
    need, n_eq = [], []
    for h in range(2):
        need.append(kf - _count_tiles(lambda j: halves[h][j] > thr_rep[h], n_tiles, hr))
        n_eq.append(_count_tiles(lambda j: halves[h][j] == thr_rep[h], n_tiles, hr))
    cut_ref[...] = jnp.full((r, 1), n, I32)
    excess = jnp.maximum(jnp.max(n_eq[0] - need[0]), jnp.max(n_eq[1] - need[1]))

    @pl.when(excess > 0.0)
    def _():
        lane = lax.broadcasted_iota(I32, (hr, LANES), 1)
        n_bits = max(1, (n - 1).bit_length())

        def cbody(it, c):
            bit = jnp.left_shift(jnp.int32(1), n_bits - 1 - it)
            out = []
            for h in range(2):
                cand = c[h] + bit
                cnt = _count_tiles(
                    lambda j: (halves[h][j] == thr_rep[h]) & (lane + j * LANES < cand), n_tiles, hr)
                out.append(jnp.where(cnt < need[h], cand, c[h]))
            return tuple(out)

        zero = jnp.zeros((hr, LANES), I32)
        cut = lax.fori_loop(0, n_bits, cbody, (zero, zero))
        cut_ref[0:hr, :] = cut[0][:, 0:1]
        cut_ref[hr:r, :] = cut[1][:, 0:1]

    return jnp.concatenate([thr_rep[0][:, 0:1], thr_rep[1][:, 0:1]], axis=0)


def _dsa_prompt_kernel(far_ref, qa_ref, iq_ref, mq_ref, kva_ref, mk_ref, win_ref, o_ref,
                       s_ref, cut_ref, *, n_sel, q_lo):
    i = q_lo + pl.program_id(1)
    tq, t_keys = s_ref.shape
    qpos = i * tq + lax.broadcasted_iota(I32, (tq, t_keys), 0)
    kpos = lax.broadcasted_iota(I32, (tq, t_keys), 1)
    causal = kpos <= qpos

    ikb = mk_ref[:, MISC_IK:MISC_IK + IDX_DIM].astype(BF16)
    score = jnp.zeros((tq, t_keys), F32)
    for n in range(IDX_HEADS):
        iqn = iq_ref[:, n * IDX_DIM:(n + 1) * IDX_DIM].astype(BF16)
        dots = _nt_dot(iqn, ikb)
        score = score + jnp.maximum(dots, 0.0) * mq_ref[:, MISC_IW + n:MISC_IW + n + 1]
    score = jnp.where(causal, score, -jnp.inf)
    key = _sort_key(score)
    thr = _topk_threshold(key, n_sel, cut_ref)
    sel = ((key > thr) | ((key == thr) & (kpos <= cut_ref[...]))) & causal

    off = pl.multiple_of(jnp.maximum(i - 1, 0) * tq, tq)
    rep = H_A // KV_A
    for h in range(H_A):
        g = h // rep
        qh = (qa_ref[:, h * HEAD_DIM:(h + 1) * HEAD_DIM] * ATTN_SCALE).astype(BF16)
        kg = kva_ref[:, g * HEAD_DIM:(g + 1) * HEAD_DIM].astype(BF16)
        vg = kva_ref[:, (KV_A + g) * HEAD_DIM:(KV_A + g + 1) * HEAD_DIM].astype(BF16)
        s_ref[...] = _nt_dot(qh, kg)
        w = win_ref[h] - far_ref[h]
        w = jnp.where(i == 0, jnp.concatenate([w[:, tq:], w[:, :tq]], axis=1), w)
        s_ref[:, pl.ds(off, 2 * tq)] += w
        logits = jnp.where(sel, s_ref[...], NEG_INF)
        m = jnp.max(logits, axis=-1, keepdims=True)
        p = jnp.exp(logits - m)
        o = jnp.dot(p.astype(BF16), vg, preferred_element_type=F32)
        o_ref[0, :, h * HEAD_DIM:(h + 1) * HEAD_DIM] = o / jnp.sum(p, axis=-1, keepdims=True)


def _causal_spans(n_blocks, tq, t):
    if t // 4 < 2 * tq or n_blocks % 4:
        return [(0, n_blocks, t)]
    quarter = n_blocks // 4
    return [(0, quarter, t // 4), (quarter, quarter, t // 2), (2 * quarter, 2 * quarter, t)]


def dsa_prompt(qa, iq, misc, kva, win, far, n_batch, t):
    nq = t // Q_TILE
    n_sel = min(TOPK_MAX, t // 4)
    outs = []
    for q_lo, n_qb, t_keys in _causal_spans(nq, Q_TILE, t):
        kb = t // t_keys

        def qmap(b, i, far, q_lo=q_lo):
            return (b * nq + q_lo + i, 0)

        grid_spec = pltpu.PrefetchScalarGridSpec(
            num_scalar_prefetch=1,
            grid=(n_batch, n_qb),
            in_specs=[pl.BlockSpec((Q_TILE, H_A * HEAD_DIM), qmap),
                      pl.BlockSpec((Q_TILE, IDX_HEADS * IDX_DIM), qmap),
                      pl.BlockSpec((Q_TILE, LANES), qmap),
                      pl.BlockSpec((t_keys, 2 * KV_A * HEAD_DIM), lambda b, i, far, kb=kb: (b * kb, 0)),
                      pl.BlockSpec((t_keys, LANES), lambda b, i, far, kb=kb: (b * kb, 0)),
                      pl.BlockSpec((H_A, Q_TILE, 2 * Q_TILE), lambda b, i, far: (0, 0, 0))],
            out_specs=pl.BlockSpec((1, Q_TILE, H_A * HEAD_DIM), lambda b, i, far: (b, i, 0)),
            scratch_shapes=[pltpu.VMEM((Q_TILE, t_keys), F32), pltpu.VMEM((Q_TILE, 1), I32)],
        )
        outs.append(pl.pallas_call(
            functools.partial(_dsa_prompt_kernel, n_sel=n_sel, q_lo=q_lo),
            grid_spec=grid_spec,
            out_shape=jax.ShapeDtypeStruct((n_batch, n_qb * Q_TILE, H_A * HEAD_DIM), F32),
            compiler_params=_cparams(("parallel", "parallel")),
            name="dsa_prompt",
        )(far, qa, iq, misc, kva, misc, win))
    return jnp.concatenate(outs, axis=1).reshape(n_batch * t, H_A * HEAD_DIM)


def _fox_prompt_kernel(q_ref, k_ref, v_ref, cum_ref, cumt_ref, o_ref, *, q_lo):
    i = q_lo + pl.program_id(2)
    hp = pl.program_id(1)
    tq = q_ref.shape[0]
    t_keys = k_ref.shape[0]
    qpos = i * tq + lax.broadcasted_iota(I32, (tq, t_keys), 0)
    kpos = lax.broadcasted_iota(I32, (tq, t_keys), 1)
    causal = kpos <= qpos
    lane = lax.broadcasted_iota(I32, cum_ref.shape, 1)
    for hh in range(2):
        sl = slice(hh * HEAD_DIM, (hh + 1) * HEAD_DIM)
        head = 2 * hp + hh
        qh = (q_ref[:, sl] * ATTN_SCALE).astype(BF16)
        kh = k_ref[:, sl].astype(BF16)
        vh = v_ref[:, sl].astype(BF16)
        cq = jnp.sum(jnp.where(lane == MISC_FL + head, cum_ref[...], 0.0), axis=-1, keepdims=True)
        ck = cumt_ref[0, pl.ds(head, 1), :]
        logits = _nt_dot(qh, kh) + cq - ck
        logits = jnp.where(causal, logits, NEG_INF)
        m = jnp.max(logits, axis=-1, keepdims=True)
        p = jnp.exp(logits - m)
        o = jnp.dot(p.astype(BF16), vh, preferred_element_type=F32)
        o_ref[0, :, sl] = o / jnp.sum(p, axis=-1, keepdims=True)


def fox_prompt(qb, kb, vb, cum, cumt, n_batch, t, tq=2 * Q_TILE):
    assert MISC_FL % 8 == 0
    nq = t // tq
    hw = 2 * HEAD_DIM
    outs = []
    for q_lo, n_qb, t_keys in _causal_spans(nq, tq, t):
        kbl = t // t_keys

        def qmap(b, h, i, q_lo=q_lo):
            return (b * nq + q_lo + i, h)

        outs.append(pl.pallas_call(
            functools.partial(_fox_prompt_kernel, q_lo=q_lo),
            grid=(n_batch, H_B // 2, n_qb),
            in_specs=[pl.BlockSpec((tq, hw), qmap),
                      pl.BlockSpec((t_keys, hw), lambda b, h, i, kbl=kbl: (b * kbl, h)),
                      pl.BlockSpec((t_keys, hw), lambda b, h, i, kbl=kbl: (b * kbl, h)),
                      pl.BlockSpec((tq, LANES), lambda b, h, i, q_lo=q_lo: (b * nq + q_lo + i, 0)),
                      pl.BlockSpec((1, H_B, t_keys), lambda b, h, i: (b, MISC_FL // H_B, 0))],
            out_specs=pl.BlockSpec((1, tq, hw), lambda b, h, i: (b, i, h)),
            out_shape=jax.ShapeDtypeStruct((n_batch, n_qb * tq, H_B * HEAD_DIM), F32),
            compiler_params=_cparams(("parallel", "parallel", "parallel")),
            name="fox_prompt",
        )(qb, kb, vb, cum, cumt))
    return jnp.concatenate(outs, axis=1).reshape(n_batch * t, H_B * HEAD_DIM)


def _split_suffix(lk, tri2):
    hi = lk.astype(BF16)
    lo = (lk - hi.astype(F32)).astype(BF16)
    return jnp.dot(jnp.concatenate([hi, lo], axis=1), tri2, preferred_element_type=F32)


def _strict_lower_twice(n):
    j = lax.broadcasted_iota(I32, (2 * n, n), 0)
    s = lax.broadcasted_iota(I32, (2 * n, n), 1)
    j = jnp.where(j >= n, j - n, j)
    return jnp.where(j > s, 1.0, 0.0).astype(BF16)


def _stick_chunk(z, v_bf16, carry, acc, tri2, mask):
    w_keys = z.shape[1]
    n_sub = w_keys // LANES
    log_keep = -(jnp.maximum(z, 0.0) + jnp.log(1.0 + jnp.exp(-jnp.abs(z))))
    log_hit = z + log_keep
    if mask is not None:
        log_keep = jnp.where(mask, log_keep, 0.0)
    subs = [log_keep[:, s * LANES:(s + 1) * LANES] for s in range(n_sub)]
    sums = [jnp.sum(x, axis=-1, keepdims=True) for x in subs]
    afters = [None] * n_sub
    right = carry
    for s in range(n_sub - 1, -1, -1):
        afters[s] = _split_suffix(subs[s], tri2) + right
        right = right + sums[s]
    w = jnp.exp(log_hit + jnp.concatenate(afters, axis=1))
    if mask is not None:
        w = jnp.where(mask, w, 0.0)
    acc = acc + jnp.dot(w.astype(BF16), v_bf16, preferred_element_type=F32)
    return right, acc


def _sb_prompt_kernel(q_ref, k_ref, v_ref, o_ref, *, chunk):
    i = pl.program_id(2)
    tq = q_ref.shape[0]
    tri2 = _strict_lower_twice(LANES)
    qs = [(q_ref[:, hh * HEAD_DIM:(hh + 1) * HEAD_DIM] * ATTN_SCALE).astype(BF16) for hh in range(2)]

    def chunk_step(c, st, masked):
        base = pl.multiple_of(c * chunk, chunk)
        mask = None
        if masked:
            qpos = i * tq + lax.broadcasted_iota(I32, (tq, chunk), 0)
            kpos = base + lax.broadcasted_iota(I32, (tq, chunk), 1)
            mask = kpos < qpos
        out = []
        for hh in range(2):
            cs = slice(hh * HEAD_DIM, (hh + 1) * HEAD_DIM)
            kh = k_ref[pl.ds(base, chunk), cs].astype(BF16)
            vh = v_ref[pl.ds(base, chunk), cs].astype(BF16)
            z = _nt_dot(qs[hh], kh)
            out.extend(_stick_chunk(z, vh, st[2 * hh], st[2 * hh + 1], tri2, mask))
        return tuple(out)

    c_diag = (i * tq) // chunk
    zero = (jnp.zeros((tq, 1), F32), jnp.zeros((tq, HEAD_DIM), F32))
    st = chunk_step(c_diag, zero + zero, True)
    st = lax.fori_loop(0, c_diag, lambda step, s_: chunk_step(c_diag - 1 - step, s_, False), st)
    o_ref[:, :HEAD_DIM] = st[1]
    o_ref[:, HEAD_DIM:] = st[3]


def sb_prompt(q, k, v, n_batch, t):
    tq = 2 * Q_TILE
    nq = t // tq
    hw = 2 * HEAD_DIM
    chunk = min(4 * LANES, t)
    assert t % tq == 0 and chunk % tq == 0
    return pl.pallas_call(
        functools.partial(_sb_prompt_kernel, chunk=chunk),
        grid=(n_batch, H_C // 2, nq),
        in_specs=[pl.BlockSpec((tq, hw), lambda b, h, i: (b * nq + i, h)),
                  pl.BlockSpec((t, hw), lambda b, h, i: (b, h)),
                  pl.BlockSpec((t, hw), lambda b, h, i: (b, h))],
        out_specs=pl.BlockSpec((tq, hw), lambda b, h, i: (b * nq + i, h)),
        out_shape=jax.ShapeDtypeStruct((n_batch * t, H_C * HEAD_DIM), F32),
        compiler_params=_cparams(("parallel", "parallel", "parallel")),
        name="sb_prompt",
    )(q, k, v)


def _cumsum_prompt_kernel(m_ref, cum_ref, cumt_ref, carry_ref):
    i = pl.program_id(1)
    tb = m_ref.shape[0]

    @pl.when(i == 0)
    def _():
        carry_ref[...] = jnp.zeros_like(carry_ref)

    r = lax.broadcasted_iota(I32, (tb, tb), 0)
    c = lax.broadcasted_iota(I32, (tb, tb), 1)
    lower = jnp.where(c <= r, 1.0, 0.0)
    cum = jnp.dot(lower, m_ref[...], precision=_HI, preferred_element_type=F32) + carry_ref[...]
    cum_ref[...] = cum
    carry_ref[...] = cum[tb - 1:tb, :]
    eye = jnp.where(lax.broadcasted_iota(I32, (LANES, LANES), 0) == lax.broadcasted_iota(I32, (LANES, LANES), 1),
                    1.0, 0.0)
    cumt_ref[0] = lax.dot_general(eye, cum, (((1,), (1,)), ((), ())), precision=_HI,
                                  preferred_element_type=F32)


def cumsum_prompt(misc, n_batch, t, tb=256):
    nb = t // tb
    return pl.pallas_call(
        _cumsum_prompt_kernel,
        grid=(n_batch, nb),
        in_specs=[pl.BlockSpec((tb, LANES), lambda b, i: (b * nb + i, 0))],
        out_specs=[pl.BlockSpec((tb, LANES), lambda b, i: (b * nb + i, 0)),
                   pl.BlockSpec((1, LANES, tb), lambda b, i: (b, 0, i))],
        out_shape=[jax.ShapeDtypeStruct((n_batch * t, LANES), F32),
                   jax.ShapeDtypeStruct((n_batch, LANES, t), F32)],
        scratch_shapes=[pltpu.VMEM((1, LANES), F32)],
        compiler_params=_cparams(("parallel", "arbitrary")),
        name="cumsum_prompt",
    )(misc)


def _cumsum_pages_kernel(pt_ref, *refs, n_group):
    page_refs = refs[:n_group]
    new_ref, cum_ref, cumnew_ref, carry_ref = refs[n_group:]
    p = pl.program_id(1)

    @pl.when(p == 0)
    def _():
        carry_ref[...] = jnp.zeros_like(carry_ref)

    r = lax.broadcasted_iota(I32, (PAGE_SIZE, PAGE_SIZE), 0)
    c = lax.broadcasted_iota(I32, (PAGE_SIZE, PAGE_SIZE), 1)
    upper = jnp.where(r <= c, 1.0, 0.0)
    carry = carry_ref[...]
    within = [jnp.dot(r_[...], upper, precision=_HI, preferred_element_type=F32) for r_ in page_refs]
    for g in range(n_group):
        cum = within[g] + carry
        cum_ref[0, :, g * PAGE_SIZE:(g + 1) * PAGE_SIZE] = cum
        carry = cum[:, PAGE_SIZE - 1:PAGE_SIZE]
    carry_ref[...] = carry

    @pl.when(p == pl.num_programs(1) - 1)
    def _():
        n_new = new_ref.shape[1]
        eye = jnp.where(lax.broadcasted_iota(I32, (H_B, H_B), 0) == lax.broadcasted_iota(I32, (H_B, H_B), 1),
                        1.0, 0.0)
        newt = lax.dot_general(eye, new_ref[0], (((1,), (1,)), ((), ())), precision=_HI,
                               preferred_element_type=F32)
        lane = lax.broadcasted_iota(I32, (H_B, LANES), 1)
        out = jnp.zeros((H_B, LANES), F32)
        run = carry
        for s in range(n_new):
            run = run + newt[:, s:s + 1]
            out = jnp.where(lane == s, run, out)
        cumnew_ref[0] = out


def cumsum_pages(logf_t, layer, page_table, logf_new, n_group=32):
    n_b, n_pages = page_table.shape
    n_group = math.gcd(n_group, n_pages)
    steps = n_pages // n_group

    def page_map(g):
        return lambda b, p, pt: (layer, pt[b, p * n_group + g], 0, 0)

    grid_spec = pltpu.PrefetchScalarGridSpec(
        num_scalar_prefetch=1,
        grid=(n_b, steps),
        in_specs=[pl.BlockSpec((None, None, H_B, PAGE_SIZE), page_map(g)) for g in range(n_group)]
        + [pl.BlockSpec((1, logf_new.shape[1], H_B), lambda b, p, pt: (b, 0, 0))],
        out_specs=[pl.BlockSpec((1, H_B, n_group * PAGE_SIZE), lambda b, p, pt: (b, 0, p)),
                   pl.BlockSpec((1, H_B, LANES), lambda b, p, pt: (b, 0, 0))],
        scratch_shapes=[pltpu.VMEM((H_B, 1), F32)],
    )
    return pl.pallas_call(
        functools.partial(_cumsum_pages_kernel, n_group=n_group),
        grid_spec=grid_spec,
        out_shape=[jax.ShapeDtypeStruct((n_b, H_B, n_pages * PAGE_SIZE), F32),
                   jax.ShapeDtypeStruct((n_b, H_B, LANES), F32)],
        compiler_params=_cparams(("parallel", "arbitrary")),
        name="cumsum_pages",
    )(page_table, *([logf_t] * n_group), logf_new)


def _xattn_kernel(q_ref, k_ref, v_ref, o_ref, *, rows):
    scale = XA_HEAD_DIM ** -0.5
    for bi in range(k_ref.shape[0]):
        rs = slice(bi * rows, (bi + 1) * rows)
        for h in range(XA_HEADS):
            cs = slice(h * XA_HEAD_DIM, (h + 1) * XA_HEAD_DIM)
            q = q_ref[rs, cs].astype(BF16)
            k = k_ref[bi, :, cs].astype(BF16)
            v = v_ref[bi, :, cs].astype(BF16)
            logits = _nt_dot(q, k) * scale
            m = jnp.max(logits, axis=-1, keepdims=True)
            p = jnp.exp(logits - m)
            p = p / jnp.sum(p, axis=-1, keepdims=True)
            o_ref[rs, cs] = jnp.dot(p.astype(BF16), v, preferred_element_type=F32)


def xattn(q, mem_k, mem_v, row0, rows, n_bblk):
    dq = q.shape[1]
    n_b, n_mem, _ = mem_k.shape
    if rows >= 8:
        tq = min(rows, 512)
        per = rows // tq
        steps = n_b * per
        m_spec = pl.BlockSpec((1, n_mem, dq), lambda i: (i // per, 0, 0))
        kern_rows = tq
    else:
        tq = rows * n_bblk
        steps = n_b // n_bblk
        m_spec = pl.BlockSpec((n_bblk, n_mem, dq), lambda i: (i, 0, 0))
        kern_rows = rows
    assert row0 % tq == 0
    return pl.pallas_call(
        functools.partial(_xattn_kernel, rows=kern_rows),
        grid=(steps,),
        in_specs=[pl.BlockSpec((tq, dq), lambda i: (row0 // tq + i, 0)), m_spec, m_spec],
        out_specs=pl.BlockSpec((tq, dq), lambda i: (i, 0)),
        out_shape=jax.ShapeDtypeStruct((n_b * rows, dq), F32),
        compiler_params=_cparams(("parallel",)),
        name="xattn",
    )(q, mem_k, mem_v)


def _router_kernel(x_ref, w_ref, b_ref, idx_ref, gate_ref, rank_ref, cnt_ref, carry_s):
    i = pl.program_id(0)

    @pl.when(i == 0)
    def _():
        carry_s[...] = jnp.zeros_like(carry_s)

    logits = jnp.dot(x_ref[...], w_ref[...], precision=_HI, preferred_element_type=F32) + b_ref[...]
    tm = logits.shape[0]
    lane = lax.broadcasted_iota(I32, logits.shape, 1)
    lane_k = lax.broadcasted_iota(I32, (tm, TOP_K), 1)
    vals = jnp.zeros((tm, TOP_K), F32)
    idxs = jnp.zeros((tm, TOP_K), I32)
    picks = []
    cur = logits
    for r in range(TOP_K):
        m = jnp.max(cur, axis=-1, keepdims=True)
        ix = jnp.min(jnp.where(cur == m, lane, N_EXPERTS), axis=-1, keepdims=True)
        vals = jnp.where(lane_k == r, m, vals)
        idxs = jnp.where(lane_k == r, ix, idxs)
        picks.append(lane == ix)
        cur = jnp.where(picks[-1], -jnp.inf, cur)
    e = jnp.exp(vals - jnp.max(vals, axis=-1, keepdims=True))
    gate_ref[...] = e / jnp.sum(e, axis=-1, keepdims=True)
    idx_ref[...] = idxs

    chosen = jnp.zeros(logits.shape, F32)
    for pk in picks:
        chosen = jnp.where(pk, 1.0, chosen)
    r_i = lax.broadcasted_iota(I32, (tm, tm), 0)
    c_i = lax.broadcasted_iota(I32, (tm, tm), 1)
    before = jnp.where(c_i < r_i, 1.0, 0.0).astype(BF16)
    base = jnp.dot(before, chosen.astype(BF16), preferred_element_type=F32) + carry_s[...]
    ranks = jnp.zeros((tm, TOP_K), F32)
    for r, pk in enumerate(picks):
        ranks = jnp.where(lane_k == r, jnp.sum(jnp.where(pk, base, 0.0), axis=-1, keepdims=True), ranks)
    rank_ref[...] = ranks.astype(I32)
    carry_s[...] += jnp.sum(chosen, axis=0, keepdims=True)
    cnt_ref[...] = carry_s[...].astype(I32)


def router(x, w_router, b_router, tm=ROW_TILE):
    m, d = x.shape
    return pl.pallas_call(
        _router_kernel,
        grid=(m // tm,),
        in_specs=[pl.BlockSpec((tm, d), lambda i: (i, 0)),
                  pl.BlockSpec((d, N_EXPERTS), lambda i: (0, 0)),
                  pl.BlockSpec((1, N_EXPERTS), lambda i: (0, 0))],
        out_specs=[pl.BlockSpec((tm, TOP_K), lambda i: (i, 0)),
                   pl.BlockSpec((tm, TOP_K), lambda i: (i, 0)),
                   pl.BlockSpec((tm, TOP_K), lambda i: (i, 0)),
                   pl.BlockSpec((1, N_EXPERTS), lambda i: (0, 0))],
        out_shape=[jax.ShapeDtypeStruct((m, TOP_K), I32), jax.ShapeDtypeStruct((m, TOP_K), F32),
                   jax.ShapeDtypeStruct((m, TOP_K), I32), jax.ShapeDtypeStruct((1, N_EXPERTS), I32)],
        scratch_shapes=[pltpu.VMEM((1, N_EXPERTS), F32)],
        compiler_params=_cparams(("arbitrary",)),
        name="router",
    )(x, w_router, b_router.reshape(1, N_EXPERTS))


def _expert_kernel(bexp_ref, bval_ref, x_ref, wgu_ref, bgu_ref, wd_ref, bd_ref, o_ref, wgu_s, wd_s):
    i = pl.program_id(0)
    prev = bexp_ref[jnp.maximum(i - 1, 0)]

    @pl.when((i == 0) | (bexp_ref[i] != prev))
    def _():
        wgu_s[...] = wgu_ref[0].astype(BF16)
        wd_s[...] = wd_ref[0].astype(BF16)

    @pl.when(bval_ref[i] > 0)
    def _():
        xb = x_ref[...].astype(BF16)
        de = wd_s.shape[0]
        g = jnp.dot(xb, wgu_s[:, :de], preferred_element_type=F32) + bgu_ref[0, :, :de]
        u = jnp.dot(xb, wgu_s[:, de:], preferred_element_type=F32) + bgu_ref[0, :, de:]
        g = jnp.minimum(g, SWIGLU_LIMIT)
        u = jnp.clip(u, -SWIGLU_LIMIT, SWIGLU_LIMIT)
        act = g * jax.nn.sigmoid(SWIGLU_ALPHA * g) * (u + 1.0)
        o_ref[...] = jnp.dot(act.astype(BF16), wd_s[...], preferred_element_type=F32) + bd_ref[0]

    @pl.when(bval_ref[i] == 0)
    def _():
        o_ref[...] = jnp.zeros_like(o_ref)


def expert_blocks(x_rows, block_exp, block_valid, w_gu, b_gu, w_down, b_down, layer):
    n_rows, d = x_rows.shape
    n_blocks = n_rows // MOE_TILE
    de = w_down.shape[2]
    grid_spec = pltpu.PrefetchScalarGridSpec(
        num_scalar_prefetch=2,
        grid=(n_blocks,),
        in_specs=[pl.BlockSpec((MOE_TILE, d), lambda i, be, bv: (i, 0)),
                  pl.BlockSpec((None, 1, d, 2 * de), lambda i, be, bv: (layer, be[i], 0, 0)),
                  pl.BlockSpec((None, 1, 1, 2 * de), lambda i, be, bv: (layer, be[i], 0, 0)),
                  pl.BlockSpec((None, 1, de, d), lambda i, be, bv: (layer, be[i], 0, 0)),
                  pl.BlockSpec((None, 1, 1, d), lambda i, be, bv: (layer, be[i], 0, 0))],
        out_specs=pl.BlockSpec((MOE_TILE, d), lambda i, be, bv: (i, 0)),
        scratch_shapes=[pltpu.VMEM((d, 2 * de), BF16), pltpu.VMEM((de, d), BF16)],
    )
    return pl.pallas_call(
        _expert_kernel,
        grid_spec=grid_spec,
        out_shape=jax.ShapeDtypeStruct((n_rows, d), F32),
        compiler_params=_cparams(("arbitrary",)),
        name="experts",
    )(block_exp, block_valid, x_rows, w_gu, b_gu.reshape(*b_gu.shape[:2], 1, 2 * de),
      w_down, b_down.reshape(*b_down.shape[:2], 1, d))


def _combine_ln_kernel(y_ref, gate_ref, x_ref, g_ref, b_ref, o_ref):
    y = y_ref[0] * gate_ref[:, 0:1]
    for k in range(1, TOP_K):
        y = y + y_ref[k] * gate_ref[:, k:k + 1]
    o_ref[...] = _layer_norm(DN_ALPHA * x_ref[...] + y, g_ref[...], b_ref[...])


def combine_ln(y_slots, gate, x, g, b, tm=ROW_TILE):
    m, d = x.shape
    return pl.pallas_call(
        _combine_ln_kernel,
        grid=(m // tm,),
        in_specs=[pl.BlockSpec((TOP_K, tm, d), lambda i: (0, i, 0)),
                  pl.BlockSpec((tm, TOP_K), lambda i: (i, 0)),
                  pl.BlockSpec((tm, d), lambda i: (i, 0)),
                  pl.BlockSpec((1, d), lambda i: (0, 0)),
                  pl.BlockSpec((1, d), lambda i: (0, 0))],
        out_specs=pl.BlockSpec((tm, d), lambda i: (i, 0)),
        out_shape=jax.ShapeDtypeStruct((m, d), F32),
        compiler_params=_cparams(("parallel",)),
        name="moe_combine_ln",
    )(y_slots, gate, x, g.reshape(1, d), b.reshape(1, d))


def moe_ln(x, w_router, b_router, w_gu, b_gu, w_down, b_down, layer, g, b):
    n_tok, d = x.shape
    top_idx, gate, rank, counts = router(x, w_router[layer], b_router[layer])
    counts = counts.reshape(N_EXPERTS)
    n_assign = n_tok * TOP_K
    padded = (counts + MOE_TILE - 1) // MOE_TILE * MOE_TILE
    pad_end = jnp.cumsum(padded)
    pad_start = pad_end - padded
    pos = pad_start[top_idx] + rank
    n_blocks = -(-n_assign // MOE_TILE) + N_EXPERTS
    n_rows = n_blocks * MOE_TILE
    tok = jnp.broadcast_to(jnp.arange(n_tok, dtype=I32)[:, None], (n_tok, TOP_K))
    row_tok = jnp.full((n_rows,), n_tok, I32).at[pos.reshape(-1)].set(tok.reshape(-1), unique_indices=True)
    block_start = jnp.arange(n_blocks, dtype=I32) * MOE_TILE
    block_exp = jnp.minimum(jnp.sum((pad_end[None, :] <= block_start[:, None]).astype(I32), axis=1), N_EXPERTS - 1)
    block_valid = (block_start < pad_end[-1]).astype(I32)
    x_pad = jnp.concatenate([x, jnp.zeros((1, d), x.dtype)], axis=0)
    y_rows = expert_blocks(x_pad[row_tok], block_exp, block_valid, w_gu, b_gu, w_down, b_down, layer)
    y_slots = y_rows[pos.T.reshape(-1)].reshape(TOP_K, n_tok, d)
    return combine_ln(y_slots, gate, x, g, b)


def _new_page_t(dst_ref, new_rows):
    dst_ref[...] = jnp.zeros_like(dst_ref)
    dst_ref[0:new_rows.shape[0], :] = new_rows
    return dst_ref[...].T


def _pages(refs):
    return jnp.concatenate([r[...] for r in refs], axis=1)


def _dsa_scores_kernel(pt_ref, *refs, n_group):
    page_refs = refs[:n_group]
    iq_ref, iw_ref, iknew_ref, sc_ref, scnew_ref, new_s = refs[n_group:]
    p = pl.program_id(1)
    iqm = iq_ref[0].astype(BF16)
    iw = iw_ref[0]
    n_q = iqm.shape[0] // IDX_HEADS

    def score_of(ik_t):
        dots = jnp.dot(iqm, ik_t.astype(BF16), preferred_element_type=F32)
        w = jnp.maximum(dots, 0.0) * iw
        return jnp.sum(w.reshape(n_q, IDX_HEADS, ik_t.shape[1]), axis=1)

    sc_ref[0] = score_of(_pages(page_refs))

    @pl.when(p == 0)
    def _():
        n_new = iknew_ref.shape[1]
        s = score_of(_new_page_t(new_s, iknew_ref[0]))
        t = lax.broadcasted_iota(I32, s.shape, 0)
        c = lax.broadcasted_iota(I32, s.shape, 1)
        scnew_ref[0] = jnp.where((c <= t) & (c < n_new), s, -jnp.inf)


def dsa_scores_decode(idxk_t, layer, page_table, iq_rows, iw_rows, ik_new, n_group=32):
    n_b, n_pages = page_table.shape
    n_group = math.gcd(n_group, n_pages)
    n_q = iq_rows.shape[1] // IDX_HEADS
    steps = n_pages // n_group

    def page_map(g):
        return lambda b, p, pt: (layer, pt[b, p * n_group + g], 0, 0)

    grid_spec = pltpu.PrefetchScalarGridSpec(
        num_scalar_prefetch=1,
        grid=(n_b, steps),
        in_specs=[pl.BlockSpec((None, None, IDX_DIM, PAGE_SIZE), page_map(g)) for g in range(n_group)]
        + [pl.BlockSpec((1,) + iq_rows.shape[1:], lambda b, p, pt: (b, 0, 0)),
           pl.BlockSpec((1,) + iw_rows.shape[1:], lambda b, p, pt: (b, 0, 0)),
           pl.BlockSpec((1,) + ik_new.shape[1:], lambda b, p, pt: (b, 0, 0))],
        out_specs=[pl.BlockSpec((1, n_q, n_group * PAGE_SIZE), lambda b, p, pt: (b, 0, p)),
                   pl.BlockSpec((1, n_q, PAGE_SIZE), lambda b, p, pt: (b, 0, 0))],
        scratch_shapes=[pltpu.VMEM((PAGE_SIZE, IDX_DIM), F32)],
    )
    return pl.pallas_call(
        functools.partial(_dsa_scores_kernel, n_group=n_group),
        grid_spec=grid_spec,
        out_shape=[jax.ShapeDtypeStruct((n_b, n_q, n_pages * PAGE_SIZE), F32),
                   jax.ShapeDtypeStruct((n_b, n_q, PAGE_SIZE), F32)],
        compiler_params=_cparams(("parallel", "arbitrary")),
        name="dsa_scores_decode",
    )(page_table, *([idxk_t] * n_group), iq_rows, iw_rows, ik_new)


def _select_kernel(s_ref, sel_ref, cut_ref, *, n_sel):
    key = _sort_key(s_ref[...])
    thr = _topk_threshold(key, n_sel, cut_ref)
    idx = lax.broadcasted_iota(I32, key.shape, 1)
    sel = (key > thr) | ((key == thr) & (idx <= cut_ref[...]))
    sel_ref[...] = jnp.where(sel, 1.0, 0.0)


def topk_select(scores, n_sel):
    return pl.pallas_call(
        functools.partial(_select_kernel, n_sel=n_sel),
        out_shape=jax.ShapeDtypeStruct(scores.shape, F32),
        scratch_shapes=[pltpu.VMEM((scores.shape[0], 1), I32)],
        compiler_params=pltpu.CompilerParams(vmem_limit_bytes=VMEM_LIMIT),
        name="topk_select",
    )(scores)


def _softmax_update(logits, mask, vt_bf16, m_s, l_s, acc_s):
    if mask is not None:
        logits = jnp.where(mask, logits, NEG_INF)
    m_old = m_s[...]
    m_new = jnp.maximum(m_old, jnp.max(logits, axis=-1, keepdims=True))
    alpha = jnp.exp(m_old - m_new)
    pe = jnp.exp(logits - m_new)
    if mask is not None:
        pe = jnp.where(mask, pe, 0.0)
    l_s[...] = alpha * l_s[...] + jnp.sum(pe, axis=-1, keepdims=True)
    acc_s[...] = alpha * acc_s[...] + _nt_dot(pe.astype(BF16), vt_bf16)
    m_s[...] = m_new


def _expand_rows(x, reps):
    n_q, s = x.shape
    return jnp.broadcast_to(x[:, None, :], (n_q, reps, s)).reshape(n_q * reps, s)


def _softmax_init(m_s, l_s, acc_s):
    m_s[...] = jnp.full_like(m_s, NEG_INF)
    l_s[...] = jnp.zeros_like(l_s)
    acc_s[...] = jnp.zeros_like(acc_s)


def _dsa_attn_decode_kernel(pt_ref, *refs, n_group):
    k_refs = refs[:n_group]
    v_refs = refs[n_group:2 * n_group]
    (qbd_ref, kvnew_ref, sel_ref, selnew_ref, bias_ref, biasnew_ref, o_ref,
     m_s, l_s, acc_s, knew_s, vnew_s) = refs[2 * n_group:]
    p = pl.program_id(1)
    kw = KV_A * HEAD_DIM

    @pl.when(p == 0)
    def _():
        _softmax_init(m_s, l_s, acc_s)

    qbd = (qbd_ref[0] * ATTN_SCALE).astype(BF16)
    logits = jnp.dot(qbd, _pages(k_refs).astype(BF16), preferred_element_type=F32) + bias_ref[...]
    mask = _expand_rows(sel_ref[0], H_A) > 0.5
    _softmax_update(logits, mask, _pages(v_refs).astype(BF16), m_s, l_s, acc_s)

    @pl.when(p == pl.num_programs(1) - 1)
    def _():
        n_new = kvnew_ref.shape[1]
        kt = _new_page_t(knew_s, kvnew_ref[0, :, :kw]).astype(BF16)
        vt = _new_page_t(vnew_s, kvnew_ref[0, :, kw:]).astype(BF16)
        lg = jnp.dot(qbd, kt, preferred_element_type=F32) + biasnew_ref[...]
        t = lax.broadcasted_iota(I32, lg.shape, 0) // H_A
        c = lax.broadcasted_iota(I32, lg.shape, 1)
        mk = (_expand_rows(selnew_ref[0], H_A) > 0.5) & (c <= t) & (c < n_new)
        _softmax_update(lg, mk, vt, m_s, l_s, acc_s)
        o_ref[0] = acc_s[...] / l_s[...]


def _decode_specs(n_group, width, page_map, rows, extra):
    page_spec = [pl.BlockSpec((None, None, width, PAGE_SIZE), page_map(g)) for g in range(n_group)]
    return page_spec + page_spec + [pl.BlockSpec((1, rows, width), lambda b, p, pt: (b, 0, 0))] + extra


def dsa_attn_decode(k_t, v_t, layer, page_table, qbd, kv_new, sel, bias, n_group=16):
    n_b, n_pages = page_table.shape
    n_group = math.gcd(n_group, n_pages)
    rows = qbd.shape[1]
    n_q = rows // H_A
    kw = KV_A * HEAD_DIM
    steps = n_pages // n_group
    gw = n_group * PAGE_SIZE

    def page_map(g):
        return lambda b, p, pt: (layer, pt[b, p * n_group + g], 0, 0)

    grid_spec = pltpu.PrefetchScalarGridSpec(
        num_scalar_prefetch=1,
        grid=(n_b, steps),
        in_specs=_decode_specs(n_group, kw, page_map, rows, [
            pl.BlockSpec((1,) + kv_new.shape[1:], lambda b, p, pt: (b, 0, 0)),
            pl.BlockSpec((1, n_q, gw), lambda b, p, pt: (b, 0, p)),
            pl.BlockSpec((1, n_q, PAGE_SIZE), lambda b, p, pt: (b, 0, n_pages)),
            pl.BlockSpec((rows, gw), lambda b, p, pt: (0, p)),
            pl.BlockSpec((rows, PAGE_SIZE), lambda b, p, pt: (0, n_pages))]),
        out_specs=pl.BlockSpec((1, rows, kw), lambda b, p, pt: (b, 0, 0)),
        scratch_shapes=[pltpu.VMEM((rows, 1), F32), pltpu.VMEM((rows, 1), F32), pltpu.VMEM((rows, kw), F32),
                        pltpu.VMEM((PAGE_SIZE, kw), F32), pltpu.VMEM((PAGE_SIZE, kw), F32)],
    )
    return pl.pallas_call(
        functools.partial(_dsa_attn_decode_kernel, n_group=n_group),
        grid_spec=grid_spec,
        out_shape=jax.ShapeDtypeStruct((n_b, rows, kw), F32),
        compiler_params=_cparams(("parallel", "arbitrary")),
        name="dsa_attn_decode",
    )(page_table, *([k_t] * n_group), *([v_t] * n_group), qbd, kv_new, sel, sel, bias, bias)


def _fox_attn_decode_kernel(pt_ref, *refs, n_group):
    k_refs = refs[:n_group]
    v_refs = refs[n_group:2 * n_group]
    (qbd_ref, knew_ref, vnew_ref, cq_ref, ck_ref, cknew_ref, o_ref,
     m_s, l_s, acc_s, knew_s, vnew_s) = refs[2 * n_group:]
    p = pl.program_id(1)
    n_q = qbd_ref.shape[1] // H_B

    @pl.when(p == 0)
    def _():
        _softmax_init(m_s, l_s, acc_s)

    qbd = (qbd_ref[0] * ATTN_SCALE).astype(BF16)
    cq = cq_ref[0]
    ck = jnp.concatenate([ck_ref[0]] * n_q, axis=0)
    logits = jnp.dot(qbd, _pages(k_refs).astype(BF16), preferred_element_type=F32) + cq - ck
    _softmax_update(logits, None, _pages(v_refs).astype(BF16), m_s, l_s, acc_s)

    @pl.when(p == pl.num_programs(1) - 1)
    def _():
        n_new = knew_ref.shape[1]
        kt = _new_page_t(knew_s, knew_ref[0]).astype(BF16)
        vt = _new_page_t(vnew_s, vnew_ref[0]).astype(BF16)
        ckn = jnp.concatenate([cknew_ref[0]] * n_q, axis=0)
        lg = jnp.dot(qbd, kt, preferred_element_type=F32) + cq - ckn
        t = lax.broadcasted_iota(I32, lg.shape, 0) // H_B
        c = lax.broadcasted_iota(I32, lg.shape, 1)
        _softmax_update(lg, (c <= t) & (c < n_new), vt, m_s, l_s, acc_s)
        o_ref[0] = acc_s[...] / l_s[...]


def fox_attn_decode(k_t, v_t, layer, page_table, qbd, k_new, v_new, cq_rows, ck, ck_new, n_group=8):
    n_b, n_pages = page_table.shape
    n_group = math.gcd(n_group, n_pages)
    rows = qbd.shape[1]
    kw = H_B * HEAD_DIM
    steps = n_pages // n_group
    gw = n_group * PAGE_SIZE

    def page_map(g):
        return lambda b, p, pt: (layer, pt[b, p * n_group + g], 0, 0)

    grid_spec = pltpu.PrefetchScalarGridSpec(
        num_scalar_prefetch=1,
        grid=(n_b, steps),
        in_specs=_decode_specs(n_group, kw, page_map, rows, [
            pl.BlockSpec((1,) + k_new.shape[1:], lambda b, p, pt: (b, 0, 0)),
            pl.BlockSpec((1,) + v_new.shape[1:], lambda b, p, pt: (b, 0, 0)),
            pl.BlockSpec((1, rows, 1), lambda b, p, pt: (b, 0, 0)),
            pl.BlockSpec((1, H_B, gw), lambda b, p, pt: (b, 0, p)),
            pl.BlockSpec((1, H_B, PAGE_SIZE), lambda b, p, pt: (b, 0, 0))]),
        out_specs=pl.BlockSpec((1, rows, kw), lambda b, p, pt: (b, 0, 0)),
        scratch_shapes=[pltpu.VMEM((rows, 1), F32), pltpu.VMEM((rows, 1), F32), pltpu.VMEM((rows, kw), F32),
                        pltpu.VMEM((PAGE_SIZE, kw), F32), pltpu.VMEM((PAGE_SIZE, kw), F32)],
    )
    return pl.pallas_call(
        functools.partial(_fox_attn_decode_kernel, n_group=n_group),
        grid_spec=grid_spec,
        out_shape=jax.ShapeDtypeStruct((n_b, rows, kw), F32),
        compiler_params=_cparams(("parallel", "arbitrary")),
        name="fox_attn_decode",
    )(page_table, *([k_t] * n_group), *([v_t] * n_group), qbd, k_new, v_new, cq_rows, ck, ck_new)


def _sb_attn_decode_kernel(pt_ref, *refs, n_group):
    k_refs = refs[:n_group]
    v_refs = refs[n_group:2 * n_group]
    qbd_ref, knew_ref, vnew_ref, o_ref, carry_s, acc_s, knew_s, vnew_s = refs[2 * n_group:]
    p = pl.program_id(1)
    qbd = (qbd_ref[0] * ATTN_SCALE).astype(BF16)
    tri2 = _strict_lower_twice(LANES)

    def chunk(kt_bf16, vt_bf16, mask):
        z = jnp.dot(qbd, kt_bf16, preferred_element_type=F32)
        w_keys = z.shape[1]
        n_sub = w_keys // LANES
        log_keep = -(jnp.maximum(z, 0.0) + jnp.log(1.0 + jnp.exp(-jnp.abs(z))))
        log_hit = z + log_keep
        if mask is not None:
            log_keep = jnp.where(mask, log_keep, 0.0)
        subs = [log_keep[:, s * LANES:(s + 1) * LANES] for s in range(n_sub)]
        sums = [jnp.sum(x, axis=-1, keepdims=True) for x in subs]
        afters = [None] * n_sub
        right = carry_s[...]
        for s in range(n_sub - 1, -1, -1):
            afters[s] = _split_suffix(subs[s], tri2) + right
            right = right + sums[s]
        w = jnp.exp(log_hit + jnp.concatenate(afters, axis=1))
        if mask is not None:
            w = jnp.where(mask, w, 0.0)
        acc_s[...] += _nt_dot(w.astype(BF16), vt_bf16)
        carry_s[...] = right

    @pl.when(p == 0)
    def _():
        carry_s[...] = jnp.zeros_like(carry_s)
        acc_s[...] = jnp.zeros_like(acc_s)
        n_new = knew_ref.shape[1]
        kt = _new_page_t(knew_s, knew_ref[0]).astype(BF16)
        vt = _new_page_t(vnew_s, vnew_ref[0]).astype(BF16)
        t = lax.broadcasted_iota(I32, (qbd.shape[0], PAGE_SIZE), 0) // H_C
        c = lax.broadcasted_iota(I32, (qbd.shape[0], PAGE_SIZE), 1)
        chunk(kt, vt, (c < t) & (c < n_new))

    chunk(_pages(k_refs).astype(BF16), _pages(v_refs).astype(BF16), None)

    @pl.when(p == pl.num_programs(1) - 1)
    def _():
        o_ref[0] = acc_s[...]


def sb_attn_decode(k_t, v_t, layer, page_table, qbd, k_new, v_new, n_group=8):
    n_b, n_pages = page_table.shape
    n_group = math.gcd(n_group, n_pages)
    rows = qbd.shape[1]
    kw = H_C * HEAD_DIM
    steps = n_pages // n_group

    def page_map(g):
        return lambda b, p, pt: (layer, pt[b, (steps - 1 - p) * n_group + g], 0, 0)

    grid_spec = pltpu.PrefetchScalarGridSpec(
        num_scalar_prefetch=1,
        grid=(n_b, steps),
        in_specs=_decode_specs(n_group, kw, page_map, rows, [
            pl.BlockSpec((1,) + k_new.shape[1:], lambda b, p, pt: (b, 0, 0)),
            pl.BlockSpec((1,) + v_new.shape[1:], lambda b, p, pt: (b, 0, 0))]),
        out_specs=pl.BlockSpec((1, rows, kw), lambda b, p, pt: (b, 0, 0)),
        scratch_shapes=[pltpu.VMEM((rows, 1), F32), pltpu.VMEM((rows, kw), F32),
                        pltpu.VMEM((PAGE_SIZE, kw), F32), pltpu.VMEM((PAGE_SIZE, kw), F32)],
    )
    return pl.pallas_call(
        functools.partial(_sb_attn_decode_kernel, n_group=n_group),
        grid_spec=grid_spec,
        out_shape=jax.ShapeDtypeStruct((n_b, rows, kw), F32),
        compiler_params=_cparams(("parallel", "arbitrary")),
        name="sb_attn_decode",
    )(page_table, *([k_t] * n_group), *([v_t] * n_group), qbd, k_new, v_new)


def _block_diag_queries(q, group):
    n_b, n_q, heads, hd = q.shape
    onehot = jax.nn.one_hot(jnp.arange(heads) // group, heads // group, dtype=q.dtype)
    out = q[:, :, :, None, :] * onehot[None, None, :, :, None]
    return out.reshape(n_b, n_q * heads, (heads // group) * hd)


def _take_diag(o, n_q, heads, group):
    n_b = o.shape[0]
    groups = heads // group
    onehot = jax.nn.one_hot(jnp.arange(heads) // group, groups, dtype=o.dtype)
    o = o.reshape(n_b, n_q, heads, groups, HEAD_DIM)
    return jnp.sum(o * onehot[None, None, :, :, None], axis=3).reshape(n_b * n_q, heads * HEAD_DIM)


def _ab_weight(w):
    sizes = (H_A * HEAD_DIM, KV_A * HEAD_DIM, KV_A * HEAD_DIM, IDX_HEADS * IDX_DIM, IDX_DIM, IDX_HEADS,
             H_B * HEAD_DIM, H_B * HEAD_DIM, H_B * HEAD_DIM, H_B)
    offs = [0]
    for sz in sizes:
        offs.append(offs[-1] + sz)
    part = [w[:, offs[i]:offs[i + 1]] for i in range(len(sizes))]
    qa, ka, va, iq, ik, iw, qb, kb, vb, fl = part
    pad = jnp.zeros((w.shape[0], LANES - IDX_DIM - IDX_HEADS - H_B), w.dtype)
    return jnp.concatenate([qa, ka, va, iq, ik, iw, fl, pad, qb, kb, vb], axis=1).astype(BF16)


def kernel(x_prompt, x_sample, cache_a_k, cache_a_v, cache_a_idxk, cache_b_k, cache_b_v, cache_b_logf, cache_c_k, cache_c_v, cache_mem_k, cache_mem_v, page_table, mem_prompt, w_in_ab, b_forget, w_out_ab, w_in_c, w_out_c, rel_bias, w_xq, w_xkv, w_xo, ln_g, ln_b, w_router, b_router, w_gate_up, b_gate_up, w_down, b_down):
    n_b, t, d = x_prompt.shape
    s_b, s_t, _ = x_sample.shape
    n_p = n_b * t
    n_s = s_b * s_t
    n_mem = mem_prompt.shape[1]
    past = page_table.shape[1] * PAGE_SIZE
    kvw = KV_A * HEAD_DIM
    xw = XA_HEADS * XA_HEAD_DIM

    x = jnp.concatenate([x_prompt.reshape(n_p, d), x_sample.reshape(n_s, d)], axis=0)
    win = bias_window(rel_bias)
    far = rel_bias[N_BUCKETS - 1]
    brow = bias_rows(rel_bias, past, s_t, past + PAGE_SIZE)
    mem_p = mem_prompt.reshape(n_b * n_mem, d)

    def pages_t(c):
        ct = jnp.moveaxis(c, 2, -1)
        return ct.reshape(ct.shape[0], ct.shape[1], -1, PAGE_SIZE)

    ak_t, av_t, ik_t = pages_t(cache_a_k), pages_t(cache_a_v), pages_t(cache_a_idxk)
    bk_t, bv_t, lf_t = pages_t(cache_b_k), pages_t(cache_b_v), pages_t(cache_b_logf)
    ck_t, cv_t = pages_t(cache_c_k), pages_t(cache_c_v)

    rows_ab_p, rows_ab_s, rows_c_p, rows_c_s, mem_kv = [], [], [], [], []
    for li in range(DEPTH):
        j = li // 2
        if li % 2 == 0:
            bf_row = jnp.zeros((1, LANES), F32).at[0, MISC_FL:MISC_FL + H_B].set(b_forget[j])
            qa, kva, iq, misc, qb, kb, vb = ab_proj(x, _ab_weight(w_in_ab[j]), bf_row)
            o_a_p = dsa_prompt(qa, iq, misc, kva, win, far, n_b, t)
            cum, cumt = cumsum_prompt(misc, n_b, t)
            o_b_p = fox_prompt(qb, kb, vb, cum, cumt, n_b, t)
            misc_s = misc[n_p:].reshape(s_b, s_t, LANES)
            ik_new = misc_s[:, :, MISC_IK:MISC_IK + IDX_DIM]
            iw_rows = misc_s[:, :, MISC_IW:MISC_IW + IDX_HEADS].reshape(s_b, s_t * IDX_HEADS, 1)
            logf_new = misc_s[:, :, MISC_FL:MISC_FL + H_B]
            iq_rows = iq[n_p:].reshape(s_b, s_t * IDX_HEADS, IDX_DIM)
            kva_s = kva[n_p:].reshape(s_b, s_t, 2 * kvw)
            sc_past, sc_new = dsa_scores_decode(ik_t, j, page_table, iq_rows, iw_rows, ik_new)
            scores = jnp.concatenate([sc_past, sc_new], axis=-1).reshape(n_s, past + PAGE_SIZE)
            sel = topk_select(scores, min(TOPK_MAX, (past + s_t) // 4)).reshape(s_b, s_t, past + PAGE_SIZE)
            qbd_a = _block_diag_queries(qa[n_p:].reshape(s_b, s_t, H_A, HEAD_DIM), H_A // KV_A)
            o_a_s = _take_diag(dsa_attn_decode(ak_t, av_t, j, page_table, qbd_a, kva_s, sel, brow),
                               s_t, H_A, H_A // KV_A)
            ck, ck_new = cumsum_pages(lf_t, j, page_table, logf_new)
            cq_rows = jnp.swapaxes(ck_new[:, :, :s_t], 1, 2).reshape(s_b, s_t * H_B, 1)
            qbd_b = _block_diag_queries(qb[n_p:].reshape(s_b, s_t, H_B, HEAD_DIM), 1)
            kb_s = kb[n_p:].reshape(s_b, s_t, H_B * HEAD_DIM)
            vb_s = vb[n_p:].reshape(s_b, s_t, H_B * HEAD_DIM)
            o_b_s = _take_diag(fox_attn_decode(bk_t, bv_t, j, page_table, qbd_b, kb_s, vb_s,
                                               cq_rows, ck, ck_new), s_t, H_B, 1)
            o_a = jnp.concatenate([o_a_p, o_a_s], axis=0)
            o_b = jnp.concatenate([o_b_p, o_b_s], axis=0)
            w_out = w_out_ab[j].astype(BF16)
            x = linear2_res_ln(o_a, w_out[:H_A * HEAD_DIM], o_b, w_out[H_A * HEAD_DIM:], x,
                               ln_g[li, 0], ln_b[li, 0])
            rows = (kva[:, :kvw], kva[:, kvw:], misc[:, MISC_IK:MISC_IK + IDX_DIM], kb, vb,
                    misc[:, MISC_FL:MISC_FL + H_B])
            tails = ((KV_A, HEAD_DIM), (KV_A, HEAD_DIM), (IDX_DIM,), (H_B, HEAD_DIM), (H_B, HEAD_DIM), (H_B,))
            rows_ab_p.append(tuple(r[:n_p].reshape(n_b, t, *tl) for r, tl in zip(rows, tails)))
            rows_ab_s.append(tuple(r[n_p:].reshape(s_b, s_t, *tl) for r, tl in zip(rows, tails)))
        else:
            q, k, v = linear(x, w_in_c[j].astype(BF16), (H_C * HEAD_DIM,) * 3, name="c_proj")
            o_p = sb_prompt(q, k, v, n_b, t)
            qbd = _block_diag_queries(q[n_p:].reshape(s_b, s_t, H_C, HEAD_DIM), 1)
            k_s = k[n_p:].reshape(s_b, s_t, H_C * HEAD_DIM)
            v_s = v[n_p:].reshape(s_b, s_t, H_C * HEAD_DIM)
            o_s = _take_diag(sb_attn_decode(ck_t, cv_t, j, page_table, qbd, k_s, v_s), s_t, H_C, 1)
            o = jnp.concatenate([o_p, o_s], axis=0)
            x = linear_res_ln(o, w_out_c[j].astype(BF16), x, ln_g[li, 0], ln_b[li, 0], name="c_out")
            rows_c_p.append((k[:n_p].reshape(n_b, t, H_C, HEAD_DIM), v[:n_p].reshape(n_b, t, H_C, HEAD_DIM)))
            rows_c_s.append((k_s.reshape(s_b, s_t, H_C, HEAD_DIM), v_s.reshape(s_b, s_t, H_C, HEAD_DIM)))

        mk_p, mv_p = linear(mem_p, w_xkv[li].astype(BF16), (xw, xw), tm=256, name="mem_proj")
        mem_kv.append((mk_p, mv_p))
        (xq,) = linear(x, w_xq[li].astype(BF16), (xw,), name="xq_proj")
        o = jnp.concatenate([
            xattn(xq, mk_p.reshape(n_b, n_mem, xw), mv_p.reshape(n_b, n_mem, xw), 0, t, 1),
            xattn(xq, cache_mem_k[li].reshape(s_b, n_mem, xw), cache_mem_v[li].reshape(s_b, n_mem, xw),
                  n_p, s_t, 8)], axis=0)
        x = linear_res_ln(o, w_xo[li].astype(BF16), x, ln_g[li, 1], ln_b[li, 1], name="xo_proj")

        x = moe_ln(x, w_router, b_router, w_gate_up, b_gate_up, w_down, b_down, li, ln_g[li, 2], ln_b[li, 2])

    def stack(rows):
        return tuple(jnp.stack([r[i] for r in rows]) for i in range(len(rows[0])))

    p_mem_k = jnp.stack([kv[0].reshape(n_b, n_mem, XA_HEADS, XA_HEAD_DIM) for kv in mem_kv])
    p_mem_v = jnp.stack([kv[1].reshape(n_b, n_mem, XA_HEADS, XA_HEAD_DIM) for kv in mem_kv])
    return ((x[:n_p].reshape(n_b, t, d), x[n_p:].reshape(s_b, s_t, d))
            + stack(rows_ab_p) + stack(rows_c_p) + (p_mem_k, p_mem_v) + stack(rows_ab_s) + stack(rows_c_s))
```
